```python
import math
import jax, jax.numpy as jnp
from jax import lax
import numpy as np

D_MODEL = 1024
BATCH = 8
SEQ = 4096
DEPTH = 4

D_MIX = D_MODEL
N_ATTN_HEADS = 8
HEAD_DIM = 64
D_ATTN = N_ATTN_HEADS * HEAD_DIM
D_REC = D_MIX - D_ATTN
N_REC_BLOCKS = 8
REC_BLOCK = D_REC // N_REC_BLOCKS
CONV_WIDTH = 4
RG_C = 8.0
D_FF = 2816
Q_BLOCK = 128
EPS = 1e-6
D_IN = 3 * D_ATTN + N_ATTN_HEADS + 2 * D_REC
SPLITS = (D_ATTN, 2 * D_ATTN, 3 * D_ATTN, 3 * D_ATTN + N_ATTN_HEADS, 3 * D_ATTN + N_ATTN_HEADS + D_REC)

kernel_name = "fox_rglru_macaron_hybrid"


def rmsnorm(x, g):
    xf = x.astype(jnp.float32)
    y = xf * lax.rsqrt(jnp.mean(xf * xf, axis=-1, keepdims=True) + EPS)
    return (y * g.astype(jnp.float32)).astype(x.dtype)


def swiglu(h, w_in, w_out):
    gu = h @ w_in
    gate, up = jnp.split(gu, 2, axis=-1)
    return (jax.nn.silu(gate) * up) @ w_out


def forgetting_attention(q, k, v, log_f):
    b, s, h, dh = q.shape
    scale = 1.0 / math.sqrt(dh)
    c = jnp.cumsum(log_f, axis=1).transpose(0, 2, 1)
    qh, kh, vh = (t.transpose(0, 2, 1, 3) for t in (q, k, v))
    outs = []
    for start in range(0, s, Q_BLOCK):
        end = start + Q_BLOCK
        qb = qh[:, :, start:end]
        kb = kh[:, :, :end]
        vb = vh[:, :, :end]
        logits = jnp.einsum('bhqd,bhkd->bhqk', qb, kb).astype(jnp.float32) * scale
        logits = logits + (c[:, :, start:end, None] - c[:, :, None, :end])
        causal = (start + jnp.arange(Q_BLOCK))[:, None] >= jnp.arange(end)[None, :]
        logits = jnp.where(causal, logits, jnp.finfo(jnp.float32).min)
        p = jax.nn.softmax(logits, axis=-1)
        outs.append(jnp.einsum('bhqk,bhkd->bhqd', p.astype(vb.dtype), vb))
    o = jnp.concatenate(outs, axis=2)
    return o.transpose(0, 2, 1, 3)


def causal_depthwise_conv(x, w, bias):
    c = x.shape[-1]
    y = lax.conv_general_dilated(
        x, w[:, None, :].astype(x.dtype), window_strides=(1,), padding=[(CONV_WIDTH - 1, 0)],
        dimension_numbers=('NWC', 'WIO', 'NWC'), feature_group_count=c)
    return y + bias


def block_diag_linear(x, w, bias):
    b, s, c = x.shape
    xb = x.reshape(b, s, N_REC_BLOCKS, REC_BLOCK)
    y = jnp.einsum('bsnc,ncd->bsnd', xb, w).reshape(b, s, c)
    return y + bias


def rg_lru(x, w_a, b_a, w_x, b_x, lam):
    r = jax.nn.sigmoid(block_diag_linear(x, w_a, b_a).astype(jnp.float32))
    i = jax.nn.sigmoid(block_diag_linear(x, w_x, b_x).astype(jnp.float32))
    log_a = -RG_C * r * jax.nn.softplus(-lam.astype(jnp.float32))
    a = jnp.exp(log_a)
    mult = jnp.sqrt(-jnp.expm1(2.0 * log_a))
    u = mult * (i * x.astype(jnp.float32))

    def combine(left, right):
        a1, b1 = left
        a2, b2 = right
        return a1 * a2, a2 * b1 + b2

    _, h = lax.associative_scan(combine, (a, u), axis=1)
    return h.astype(x.dtype)


def hybrid_mixer(h, w_in, b_f, conv_w, conv_b, w_rg_a, b_rg_a, w_rg_x, b_rg_x, rg_lambda, w_out):
    b, s, _ = h.shape
    z = h @ w_in
    q, k, v, f_logit, xr, gr = jnp.split(z, SPLITS, axis=-1)
    q = q.reshape(b, s, N_ATTN_HEADS, HEAD_DIM)
    k = k.reshape(b, s, N_ATTN_HEADS, HEAD_DIM)
    v = v.reshape(b, s, N_ATTN_HEADS, HEAD_DIM)
    log_f = jax.nn.log_sigmoid((f_logit + b_f).astype(jnp.float32))
    y_attn = forgetting_attention(q, k, v, log_f).reshape(b, s, D_ATTN)
    xr = causal_depthwise_conv(xr, conv_w, conv_b)
    y_rec = rg_lru(xr, w_rg_a, b_rg_a, w_rg_x, b_rg_x, rg_lambda) * jax.nn.gelu(gr)
    return jnp.concatenate([y_attn, y_rec], axis=-1) @ w_out


def setup_inputs(seed: int = 0) -> dict:
    key = jax.random.key(seed)
    ks = jax.random.split(key, 16)
    f32 = jnp.float32
    x = jax.random.normal(ks[0], (BATCH, SEQ, D_MODEL), f32)
    norm_g = 1.0 + 0.02 * jax.random.normal(ks[1], (DEPTH, 3, D_MODEL), f32)
    w_in = jax.random.normal(ks[2], (DEPTH, D_MODEL, D_IN), f32) * D_MODEL ** -0.5
    b_f = 3.0 + 0.1 * jax.random.normal(ks[3], (DEPTH, N_ATTN_HEADS), f32)
    conv_w = jax.random.normal(ks[4], (DEPTH, CONV_WIDTH, D_REC), f32) * CONV_WIDTH ** -0.5
    conv_b = 0.01 * jax.random.normal(ks[5], (DEPTH, D_REC), f32)
    w_rg_a = jax.random.normal(ks[6], (DEPTH, N_REC_BLOCKS, REC_BLOCK, REC_BLOCK), f32) * REC_BLOCK ** -0.5
    b_rg_a = 0.01 * jax.random.normal(ks[7], (DEPTH, D_REC), f32)
    w_rg_x = jax.random.normal(ks[8], (DEPTH, N_REC_BLOCKS, REC_BLOCK, REC_BLOCK), f32) * REC_BLOCK ** -0.5
    b_rg_x = 0.01 * jax.random.normal(ks[9], (DEPTH, D_REC), f32)
    a_c = jax.random.uniform(ks[10], (DEPTH, D_REC), f32, 0.9, 0.999)
    s_base = a_c ** (1.0 / RG_C)
    rg_lambda = jnp.log(s_base) - jnp.log1p(-s_base)
    w_out = jax.random.normal(ks[11], (DEPTH, D_MIX, D_MODEL), f32) * D_MIX ** -0.5
    w_ffn_in = jax.random.normal(ks[12], (DEPTH, 2, D_MODEL, 2 * D_FF), f32) * D_MODEL ** -0.5
    w_ffn_out = jax.random.normal(ks[13], (DEPTH, 2, D_FF, D_MODEL), f32) * D_FF ** -0.5
    final_g = 1.0 + 0.02 * jax.random.normal(ks[14], (D_MODEL,), f32)
    return {"x": x, "norm_g": norm_g, "w_in": w_in, "b_f": b_f, "conv_w": conv_w, "conv_b": conv_b,
            "w_rg_a": w_rg_a, "b_rg_a": b_rg_a, "w_rg_x": w_rg_x, "b_rg_x": b_rg_x,
            "rg_lambda": rg_lambda, "w_out": w_out, "w_ffn_in": w_ffn_in, "w_ffn_out": w_ffn_out,
            "final_g": final_g}


def reference(x, norm_g, w_in, b_f, conv_w, conv_b, w_rg_a, b_rg_a, w_rg_x, b_rg_x,
              rg_lambda, w_out, w_ffn_in, w_ffn_out, final_g):
    for l in range(DEPTH):
        x = x + 0.5 * swiglu(rmsnorm(x, norm_g[l, 0]), w_ffn_in[l, 0], w_ffn_out[l, 0])
        x = x + hybrid_mixer(rmsnorm(x, norm_g[l, 1]), w_in[l], b_f[l], conv_w[l], conv_b[l],
                             w_rg_a[l], b_rg_a[l], w_rg_x[l], b_rg_x[l], rg_lambda[l], w_out[l])
        x = x + 0.5 * swiglu(rmsnorm(x, norm_g[l, 2]), w_ffn_in[l, 1], w_ffn_out[l, 1])
    return rmsnorm(x, final_g)
```

```python
import functools
import math

import jax
import jax.numpy as jnp
from jax import lax
from jax.experimental import pallas as pl
from jax.experimental.pallas import tpu as pltpu

EPS = 1e-6
RG_C = 8.0
LANES = 128
SUBLANES = 8
NEG_BIG = -1e30
VMEM_LIMIT = 52 * 1024 * 1024

F32 = jnp.float32
BF16 = jnp.bfloat16


def _rmsnorm(x, g):
    ms = jnp.mean(x * x, axis=-1, keepdims=True)
    return x * lax.rsqrt(ms + EPS) * g


def _sigmoid(x):
    return 1.0 / (1.0 + jnp.exp(-x))


def _softplus(x):
    return jnp.maximum(x, 0.0) + jnp.log1p(jnp.exp(-jnp.abs(x)))


def _gelu_tanh(x):
    c = math.sqrt(2.0 / math.pi)
    return 0.5 * x * (1.0 + jnp.tanh(c * (x + 0.044715 * (x * x * x))))


def _shift_rows(x, k, fill):
    rows = lax.broadcasted_iota(jnp.int32, x.shape, 0)
    return jnp.where(rows >= k, pltpu.roll(x, k, 0), fill)


def _mixer_in_kernel(x_ref, g_ref, w_ref, bf_ref, cw_ref, cb_ref, wa_ref, ba_ref,
                     wx_ref, bx_ref, lam_ref,
                     q_ref, k_ref, v_ref, ct_ref, yr_ref,
                     c_carry, h_carry, x_carry, *, d_attn, d_rec, scale):
    j = pl.program_id(1)
    tm = x_ref.shape[1]

    @pl.when(j == 0)
    def _():
        c_carry[...] = jnp.zeros_like(c_carry)
        h_carry[...] = jnp.zeros_like(h_carry)
        x_carry[...] = jnp.zeros_like(x_carry)

    hn = _rmsnorm(x_ref[0], g_ref[...]).astype(BF16)
    z = jnp.dot(hn, w_ref[...], preferred_element_type=F32)

    q_ref[0] = (z[:, :d_attn] * scale).astype(BF16)
    k_ref[0] = z[:, d_attn:2 * d_attn].astype(BF16)
    v_ref[0] = z[:, 2 * d_attn:3 * d_attn].astype(BF16)
    o_xr = 3 * d_attn
    o_gr = o_xr + d_rec
    o_f = o_gr + d_rec
    xr = z[:, o_xr:o_gr]
    gr = z[:, o_gr:o_f]
    f_logit = z[:, o_f:] + bf_ref[...]

    cum = -_softplus(-f_logit)
    k = 1
    while k < tm:
        cum = cum + _shift_rows(cum, k, 0.0)
        k *= 2
    cum = cum + c_carry[...]
    c_carry[...] = cum[tm - 1:tm, :]
    ct_ref[0] = jnp.transpose(cum)[:SUBLANES, :]

    width = cw_ref.shape[0]
    ext = jnp.concatenate([x_carry[...], xr], axis=0)
    x_carry[...] = xr[tm - SUBLANES:, :]
    xc = xr * cw_ref[width - 1:width, :] + cb_ref[...]
    for s in range(1, width):
        xc = xc + pltpu.roll(ext, s, 0)[SUBLANES:, :] * cw_ref[width - 1 - s:width - s, :]

    xcb = xc.astype(BF16)
    r = _sigmoid(jnp.dot(xcb, wa_ref[...], preferred_element_type=F32) + ba_ref[...])
    i = _sigmoid(jnp.dot(xcb, wx_ref[...], preferred_element_type=F32) + bx_ref[...])
    log_a = (-RG_C * _softplus(-lam_ref[...])) * r
    a = jnp.exp(log_a)
    u = jnp.sqrt(-jnp.tanh(log_a) * (1.0 + a * a)) * (i * xc)

    k = 1
    while k < tm:
        u = u + a * _shift_rows(u, k, 0.0)
        a = a * _shift_rows(a, k, 1.0)
        k *= 2
    h = a * h_carry[...] + u
    h_carry[...] = h[tm - 1:tm, :]

    yr_ref[0] = (h * _gelu_tanh(gr)).astype(BF16)


def _mixer_in(x, g, wcat, bf_pad, conv_w, conv_b, wa, ba, wx, bx, lam, *, d_attn, d_rec, scale, tm):
    b, s, d = x.shape
    n_cols = wcat.shape[1]
    const = lambda *_: (0, 0)
    row3 = lambda bi, j: (bi, j, 0)
    kern = functools.partial(_mixer_in_kernel, d_attn=d_attn, d_rec=d_rec, scale=scale)
    return pl.pallas_call(
        kern,
        grid=(b, s // tm),
        in_specs=[
            pl.BlockSpec((1, tm, d), row3),
            pl.BlockSpec((1, d), const),
            pl.BlockSpec((d, n_cols), const),
            pl.BlockSpec((1, LANES), const),
            pl.BlockSpec(conv_w.shape, const),
            pl.BlockSpec((1, d_rec), const),
            pl.BlockSpec((d_rec, d_rec), const),
            pl.BlockSpec((1, d_rec), const),
            pl.BlockSpec((d_rec, d_rec), const),
            pl.BlockSpec((1, d_rec), const),
            pl.BlockSpec((1, d_rec), const),
        ],
        out_specs=[
            pl.BlockSpec((1, tm, d_attn), row3),
            pl.BlockSpec((1, tm, d_attn), row3),
            pl.BlockSpec((1, tm, d_attn), row3),
            pl.BlockSpec((1, SUBLANES, tm), lambda bi, j: (bi, 0, j)),
            pl.BlockSpec((1, tm, d_rec), row3),
        ],
        out_shape=[
            jax.ShapeDtypeStruct((b, s, d_attn), BF16),
            jax.ShapeDtypeStruct((b, s, d_attn), BF16),
            jax.ShapeDtypeStruct((b, s, d_attn), BF16),
            jax.ShapeDtypeStruct((b, SUBLANES, s), F32),
            jax.ShapeDtypeStruct((b, s, d_rec), BF16),
        ],
        scratch_shapes=[
            pltpu.VMEM((1, LANES), F32),
            pltpu.VMEM((1, d_rec), F32),
            pltpu.VMEM((SUBLANES, d_rec), F32),
        ],
        compiler_params=pltpu.CompilerParams(
            dimension_semantics=("parallel", "arbitrary"), vmem_limit_bytes=VMEM_LIMIT),
        name="mixer_in",
    )(x, g, wcat, bf_pad, conv_w, conv_b, wa, ba, wx, bx, lam)


def _attn_kernel(q_ref, k_ref, v_ref, c_ref, o_ref, va_ref, vb_ref, acc_ref, m_ref, *, tq, hd):
    s_len = q_ref.shape[1]
    nq = s_len // tq
    rep = tq // LANES
    is_a = lax.broadcasted_iota(jnp.int32, (1, LANES), 1) < hd

    v2 = v_ref[0]
    ones = jnp.ones_like(v2)
    va_ref[...] = jnp.where(is_a, v2, ones)
    vb_ref[...] = jnp.where(is_a, ones, v2)
    v_aug = (va_ref, vb_ref)

    row = lax.broadcasted_iota(jnp.int32, (tq, tq), 0)
    col = lax.broadcasted_iota(jnp.int32, (tq, tq), 1)
    causal = col <= row

    def q_body(qi, carry):
        q0 = pl.multiple_of(qi * tq, tq)
        q2 = q_ref[0, pl.ds(q0, tq), :]
        zero = jnp.zeros_like(q2)
        q_heads = (jnp.where(is_a, q2, zero), jnp.where(is_a, zero, q2))
        m_ref[...] = jnp.full(m_ref.shape, NEG_BIG, F32)
        acc_ref[...] = jnp.zeros(acc_ref.shape, F32)

        def tile(k0, masked):
            k2 = k_ref[0, pl.ds(k0, tq), :]
            for h in range(2):
                s = lax.dot_general(q_heads[h], k2, (((1,), (1,)), ((), ())),
                                    preferred_element_type=F32)
                s = s - c_ref[0, 0, h:h + 1, pl.ds(k0, tq)]
                if masked:
                    s = jnp.where(causal, s, NEG_BIG)
                m_old = m_ref[h]
                m_new = jnp.maximum(m_old, jnp.max(s, axis=1, keepdims=True))
                alpha = jnp.exp(m_old - m_new)
                p = jnp.exp(s - pltpu.repeat(m_new, rep, axis=1))
                pv = jnp.dot(p.astype(BF16), v_aug[h][pl.ds(k0, tq), :],
                             preferred_element_type=F32)
                acc_ref[h] = alpha * acc_ref[h] + pv
                m_ref[h] = m_new

        def k_body(kj, c):
            tile(pl.multiple_of(kj * tq, tq), False)
            return c

        lax.fori_loop(0, qi, k_body, 0)
        tile(q0, True)

        acc_a = acc_ref[0]
        acc_b = acc_ref[1]
        out = jnp.where(is_a, acc_a / pltpu.roll(acc_a, hd, 1), acc_b / pltpu.roll(acc_b, hd, 1))
        o_ref[0, pl.ds(q0, tq), :] = out.astype(BF16)
        return carry

    lax.fori_loop(0, nq, q_body, 0)


def _attention(q, k, v, c_pairs, *, hd, tq):
    b, s, d_attn = q.shape
    n_pairs = d_attn // LANES
    blk = lambda bi, p: (bi, 0, p)
    kern = functools.partial(_attn_kernel, tq=tq, hd=hd)
    return pl.pallas_call(
        kern,
        grid=(b, n_pairs),
        in_specs=[
            pl.BlockSpec((1, s, LANES), blk),
            pl.BlockSpec((1, s, LANES), blk),
            pl.BlockSpec((1, s, LANES), blk),
            pl.BlockSpec((1, 1, 2, s), lambda bi, p: (bi, p, 0, 0)),
        ],
        out_specs=pl.BlockSpec((1, s, LANES), blk),
        out_shape=jax.ShapeDtypeStruct((b, s, d_attn), BF16),
        scratch_shapes=[
            pltpu.VMEM((s, LANES), BF16),
            pltpu.VMEM((s, LANES), BF16),
            pltpu.VMEM((2, tq, LANES), F32),
            pltpu.VMEM((2, tq, LANES), F32),
        ],
        compiler_params=pltpu.CompilerParams(
            dimension_semantics=("parallel", "parallel"), vmem_limit_bytes=VMEM_LIMIT),
        name="fox_attn",
    )(q, k, v, c_pairs)


def _ffn_kernel(*refs, has_proj, final_norm, n_chunks):
    refs = list(refs)
    x_ref = refs.pop(0)
    if has_proj:
        ya_ref, yr_ref, wo_ref = refs.pop(0), refs.pop(0), refs.pop(0)
    g_ref, win_ref, wout_ref = refs.pop(0), refs.pop(0), refs.pop(0)
    if final_norm:
        gf_ref = refs.pop(0)
    o_ref = refs.pop(0)

    x = x_ref[...]
    if has_proj:
        y = jnp.concatenate([ya_ref[...], yr_ref[...]], axis=1)
        x = x + jnp.dot(y, wo_ref[...], preferred_element_type=F32)
    hn = _rmsnorm(x, g_ref[...]).astype(BF16)
    d_ff = wout_ref.shape[0]
    fc = d_ff // n_chunks
    acc = None
    for c in range(n_chunks):
        gate = jnp.dot(hn, win_ref[:, c * fc:(c + 1) * fc], preferred_element_type=F32)
        up = jnp.dot(hn, win_ref[:, d_ff + c * fc:d_ff + (c + 1) * fc], preferred_element_type=F32)
        act = (gate * _sigmoid(gate) * up).astype(BF16)
        part = jnp.dot(act, wout_ref[c * fc:(c + 1) * fc, :], preferred_element_type=F32)
        acc = part if acc is None else acc + part
    x = x + 0.5 * acc
    if final_norm:
        x = _rmsnorm(x, gf_ref[...])
    o_ref[...] = x


def _ffn(x, g, w_in, w_out, *, proj=None, final_g=None, tm, n_chunks):
    t, d = x.shape
    d_ff = w_out.shape[0]
    const = lambda i: (0, 0)
    rows = lambda i: (i, 0)
    resident = functools.partial(pl.BlockSpec, index_map=const, pipeline_mode=pl.Buffered(1))
    args = [x]
    in_specs = [pl.BlockSpec((tm, d), rows)]
    if proj is not None:
        ya, yr, wo = proj
        args += [ya, yr, wo]
        in_specs += [pl.BlockSpec((tm, ya.shape[1]), rows), pl.BlockSpec((tm, yr.shape[1]), rows),
                     resident(wo.shape)]
    args += [g, w_in, w_out]
    in_specs += [pl.BlockSpec((1, d), const), resident(w_in.shape), resident(w_out.shape)]
    if final_g is not None:
        args.append(final_g)
        in_specs.append(pl.BlockSpec((1, d), const))
    kern = functools.partial(_ffn_kernel, has_proj=proj is not None,
                             final_norm=final_g is not None, n_chunks=n_chunks)
    return pl.pallas_call(
        kern,
        grid=(t // tm,),
        in_specs=in_specs,
        out_specs=pl.BlockSpec((tm, d), rows),
        out_shape=jax.ShapeDtypeStruct((t, d), F32),
        compiler_params=pltpu.CompilerParams(
            dimension_semantics=("parallel",), vmem_limit_bytes=VMEM_LIMIT),
        name="ffn",
    )(*args)


def _pick_tile(n, target):
    t = min(n, target)
    while n % t:
        t //= 2
    return t


def _ffn_chunks(d_ff):
    return 2 if d_ff % (2 * LANES) == 0 else 1


def kernel(x, norm_g, w_in, b_f, conv_w, conv_b, w_rg_a, b_rg_a, w_rg_x, b_rg_x, rg_lambda,
           w_out, w_ffn_in, w_ffn_out, final_g):
    b, s, d = x.shape
    depth = norm_g.shape[0]
    n_heads = b_f.shape[-1]
    d_rec = conv_w.shape[-1]
    n_blocks, rec_block = w_rg_a.shape[1], w_rg_a.shape[2]
    d_attn = (w_in.shape[-1] - n_heads - 2 * d_rec) // 3
    hd = d_attn // n_heads
    d_ff = w_ffn_out.shape[2]
    assert 2 * hd == LANES and n_heads % 2 == 0 and n_heads <= SUBLANES
    assert d_attn % LANES == 0 and d_rec % LANES == 0 and d % LANES == 0

    tm_mix = _pick_tile(s, 512)
    tq = _pick_tile(s, 512)
    tm_ffn = _pick_tile(b * s, 512)
    n_chunks = _ffn_chunks(d_ff)

    o_f = 3 * d_attn
    o_xr = o_f + n_heads
    w_f = jnp.pad(w_in[:, :, o_f:o_xr], ((0, 0), (0, 0), (0, LANES - n_heads)))
    wcat = jnp.concatenate([w_in[:, :, :o_f], w_in[:, :, o_xr:], w_f], axis=-1).astype(BF16)
    bf_pad = jnp.pad(b_f, ((0, 0), (0, LANES - n_heads)))[:, None, :]
    eye = jnp.eye(n_blocks, dtype=F32)[None, :, None, :, None]

    def block_diag(w):
        return (w[:, :, :, None, :] * eye).reshape(depth, d_rec, d_rec).astype(BF16)

    wa_bd = block_diag(w_rg_a)
    wx_bd = block_diag(w_rg_x)
    w_out_b = w_out.astype(BF16)
    w_ffn_in_b = w_ffn_in.astype(BF16)
    w_ffn_out_b = w_ffn_out.astype(BF16)
    row = lambda a: a[:, None, :]
    conv_b2, ba2, bx2, lam2 = row(conv_b), row(b_rg_a), row(b_rg_x), row(rg_lambda)

    xf = x.reshape(b * s, d)
    for l in range(depth):
        g = norm_g[l][:, None, :]
        xf = _ffn(xf, g[0], w_ffn_in_b[l, 0], w_ffn_out_b[l, 0], tm=tm_ffn, n_chunks=n_chunks)
        q, k, v, ct, y_rec = _mixer_in(
            xf.reshape(b, s, d), g[1], wcat[l], bf_pad[l], conv_w[l], conv_b2[l],
            wa_bd[l], ba2[l], wx_bd[l], bx2[l], lam2[l],
            d_attn=d_attn, d_rec=d_rec, scale=1.0 / math.sqrt(hd), tm=tm_mix)
        c_pairs = ct[:, :n_heads, :].reshape(b, n_heads // 2, 2, s)
        y_attn = _attention(q, k, v, c_pairs, hd=hd, tq=tq)
        last = l == depth - 1
        xf = _ffn(xf, g[2], w_ffn_in_b[l, 1], w_ffn_out_b[l, 1],
                  proj=(y_attn.reshape(b * s, d_attn), y_rec.reshape(b * s, d_rec), w_out_b[l]),
                  final_g=final_g[None, :] if last else None, tm=tm_ffn, n_chunks=n_chunks)
    return xf.reshape(b, s, d)
```

```python
import functools
import math

import jax
import jax.numpy as jnp
from jax import lax
from jax.experimental import pallas as pl
from jax.experimental.pallas import tpu as pltpu

EPS = 1e-6
RG_C = 8.0
LANES = 128
SUBLANES = 8
BF16_ROWS = 16
NEG_BIG = -1e30
LOG2E = math.log2(math.e)
VMEM_LIMIT = 52 * 1024 * 1024

F32 = jnp.float32
BF16 = jnp.bfloat16
NT_DIMS = (((1,), (1,)), ((), ()))


def _rmsnorm(x, g):
    ms = jnp.mean(x * x, axis=-1, keepdims=True)
    return x * lax.rsqrt(ms + EPS) * g


def _sigmoid(x):
    return 1.0 / (1.0 + jnp.exp(-x))


def _softplus(x):
    return jnp.maximum(x, 0.0) + jnp.log1p(jnp.exp(-jnp.abs(x)))


def _gelu_tanh(x):
    c = math.sqrt(2.0 / math.pi)
    return 0.5 * x * (1.0 + jnp.tanh(c * (x + 0.044715 * (x * x * x))))


def _shift_rows(x, k, fill):
    rows = lax.broadcasted_iota(jnp.int32, x.shape, 0)
    return jnp.where(rows >= k, pltpu.roll(x, k, 0), fill)


def _split3(x):
    hi = x.astype(BF16).astype(F32)
    r = x - hi
    mid = r.astype(BF16).astype(F32)
    lo = (r - mid).astype(BF16).astype(F32)
    return hi, mid, lo


def _mixer_in_kernel(x_ref, g_ref, w_ref, wt_ref, bf_ref, tri_ref, cw_ref, cb_ref, wa_ref, ba_ref,
                     wx_ref, bx_ref, lam_ref,
                     q_ref, kt_ref, v_ref, ct_ref, yr_ref,
                     c_carry, h_carry, x_carry, a_s, u_s, hs_s, *, d_attn, d_rec, scale):
    j = pl.program_id(1)
    tm = x_ref.shape[1]
    nb = tm // SUBLANES

    @pl.when(j == 0)
    def _():
        c_carry[...] = jnp.zeros_like(c_carry)
        h_carry[...] = jnp.zeros_like(h_carry)
        x_carry[...] = jnp.zeros_like(x_carry)

    hn = _rmsnorm(x_ref[0], g_ref[...]).astype(BF16)
    z = jnp.dot(hn, w_ref[...], preferred_element_type=F32)
    zt = lax.dot_general(wt_ref[...], hn, NT_DIMS, preferred_element_type=F32)

    q_ref[0] = (z[:, :d_attn] * scale).astype(BF16)
    v_ref[0] = z[:, d_attn:2 * d_attn].astype(BF16)
    kt_ref[0] = zt[:d_attn].astype(BF16)
    o_xr = 2 * d_attn
    o_gr = o_xr + d_rec
    xr = z[:, o_xr:o_gr]
    gr = z[:, o_gr:]

    f_logit = zt[d_attn:d_attn + SUBLANES] + pltpu.repeat(bf_ref[...], tm // LANES, axis=1)
    log_f = -_softplus(-f_logit)
    parts = jnp.concatenate(_split3(log_f) + (jnp.zeros_like(log_f),), axis=0).astype(BF16)
    cs = jnp.dot(parts, tri_ref[...], preferred_element_type=F32)
    cs = cs[0:SUBLANES] + cs[SUBLANES:2 * SUBLANES] + cs[2 * SUBLANES:3 * SUBLANES]
    carry = c_carry[...]
    ct_ref[0] = cs[:, :tm] + pltpu.repeat(carry, tm // LANES, axis=1)
    c_carry[...] = carry + cs[:, tm:]

    width = cw_ref.shape[0]
    ext = jnp.concatenate([x_carry[...], xr], axis=0)
    x_carry[...] = xr[tm - SUBLANES:, :]
    xc = xr * cw_ref[width - 1:width, :] + cb_ref[...]
    for s in range(1, width):
        xc = xc + pltpu.roll(ext, s, 0)[SUBLANES:, :] * cw_ref[width - 1 - s:width - s, :]

    xcb = xc.astype(BF16)
    r = _sigmoid(jnp.dot(xcb, wa_ref[...], preferred_element_type=F32) + ba_ref[...])
    i = _sigmoid(jnp.dot(xcb, wx_ref[...], preferred_element_type=F32) + bx_ref[...])
    log_a = (-RG_C * _softplus(-lam_ref[...])) * r
    a = jnp.exp(log_a)
    u = jnp.sqrt(-jnp.tanh(log_a) * (1.0 + a * a)) * (i * xc)

    a3 = a.reshape(nb, SUBLANES, d_rec)
    u3 = u.reshape(nb, SUBLANES, d_rec)
    sub = lax.broadcasted_iota(jnp.int32, a3.shape, 1)
    k = 1
    while k < SUBLANES:
        keep = sub >= k
        u3 = u3 + a3 * jnp.where(keep, pltpu.roll(u3, k, 1), 0.0)
        a3 = a3 * jnp.where(keep, pltpu.roll(a3, k, 1), 1.0)
        k *= 2
    a2 = a3.reshape(tm, d_rec)
    u2 = u3.reshape(tm, d_rec)
    n_lt = d_rec // LANES
    for c in range(n_lt):
        a_s[c] = a2[:, c * LANES:(c + 1) * LANES]
        u_s[c] = u2[:, c * LANES:(c + 1) * LANES]
    last_rows = pl.ds(SUBLANES - 1, nb, stride=SUBLANES)
    ab = jnp.concatenate([a_s[c, last_rows, :] for c in range(n_lt)], axis=1)
    ub = jnp.concatenate([u_s[c, last_rows, :] for c in range(n_lt)], axis=1)
    k = 1
    while k < nb:
        ub = ub + ab * _shift_rows(ub, k, 0.0)
        ab = ab * _shift_rows(ab, k, 1.0)
        k *= 2
    h_in = h_carry[...]
    h_end = ab * h_in + ub
    h_carry[...] = h_end[nb - 1:nb, :]
    hs_s[...] = _shift_rows(h_end, 1, h_in)
    for b in range(nb):
        rows = slice(b * SUBLANES, (b + 1) * SUBLANES)
        for c in range(n_lt):
            u_s[c, rows, :] = (a_s[c, rows, :] * hs_s[b:b + 1, c * LANES:(c + 1) * LANES]
                               + u_s[c, rows, :])
    h = jnp.concatenate([u_s[c] for c in range(n_lt)], axis=1)

    yr_ref[0] = (h * _gelu_tanh(gr)).astype(BF16)


def _mixer_in(x, g, w, wt, bf_rep, tri, conv_w, conv_b, wa, ba, wx, bx, lam, *, d_attn, d_rec, scale, tm):
    b, s, d = x.shape
    const = lambda *_: (0, 0)
    row3 = lambda bi, j: (bi, j, 0)
    col3 = lambda bi, j: (bi, 0, j)
    kern = functools.partial(_mixer_in_kernel, d_attn=d_attn, d_rec=d_rec, scale=scale)
    return pl.pallas_call(
        kern,
        grid=(b, s // tm),
        in_specs=[
            pl.BlockSpec((1, tm, d), row3),
            pl.BlockSpec((1, d), const),
            pl.BlockSpec(w.shape, const),
            pl.BlockSpec(wt.shape, const),
            pl.BlockSpec((SUBLANES, LANES), const),
            pl.BlockSpec(tri.shape, const),
            pl.BlockSpec(conv_w.shape, const),
            pl.BlockSpec((1, d_rec), const),
            pl.BlockSpec((d_rec, d_rec), const),
            pl.BlockSpec((1, d_rec), const),
            pl.BlockSpec((d_rec, d_rec), const),
            pl.BlockSpec((1, d_rec), const),
            pl.BlockSpec((1, d_rec), const),
        ],
        out_specs=[
            pl.BlockSpec((1, tm, d_attn), row3),
            pl.BlockSpec((1, d_attn, tm), col3),
            pl.BlockSpec((1, tm, d_attn), row3),
            pl.BlockSpec((1, SUBLANES, tm), col3),
            pl.BlockSpec((1, tm, d_rec), row3),
        ],
        out_shape=[
            jax.ShapeDtypeStruct((b, s, d_attn), BF16),
            jax.ShapeDtypeStruct((b, d_attn, s), BF16),
            jax.ShapeDtypeStruct((b, s, d_attn), BF16),
            jax.ShapeDtypeStruct((b, SUBLANES, s), F32),
            jax.ShapeDtypeStruct((b, s, d_rec), BF16),
        ],
        scratch_shapes=[
            pltpu.VMEM((SUBLANES, LANES), F32),
            pltpu.VMEM((1, d_rec), F32),
            pltpu.VMEM((SUBLANES, d_rec), F32),
            pltpu.VMEM((d_rec // LANES, tm, LANES), F32),
            pltpu.VMEM((d_rec // LANES, tm, LANES), F32),
            pltpu.VMEM((tm // SUBLANES, d_rec), F32),
        ],
        compiler_params=pltpu.CompilerParams(
            dimension_semantics=("parallel", "arbitrary"), vmem_limit_bytes=VMEM_LIMIT),
        name="mixer_in",
    )(x, g, w, wt, bf_rep, tri, conv_w, conv_b, wa, ba, wx, bx, lam)


def _attn_kernel(q_ref, kt_ref, v_ref, c_ref, o_ref, ka_ref, kb_ref, vbd_ref, acc_ref, m_ref,
                 se_ref, so_ref, *, tq, hd):
    s_len = q_ref.shape[1]
    nq = s_len // tq
    half = tq // 2
    lane = lax.broadcasted_iota(jnp.int32, (1, LANES), 1)
    is_a = lane < hd

    sub = lax.broadcasted_iota(jnp.int32, (BF16_ROWS, s_len), 0)

    def bias_rows(h):
        hi, mid, lo = _split3(c_ref[0, 0, h:h + 1, :] * (-LOG2E))
        rows = jnp.where(sub == 0, hi, jnp.where(sub == 1, mid, jnp.where(sub == 2, lo, 0.0)))
        return rows.astype(BF16)

    zeros_k = jnp.zeros((hd - BF16_ROWS, s_len), BF16)
    ka_ref[0:hd, :] = kt_ref[0, 0:hd, :]
    ka_ref[hd:hd + BF16_ROWS, :] = bias_rows(0)
    ka_ref[hd + BF16_ROWS:, :] = zeros_k
    kb_ref[0:BF16_ROWS, :] = bias_rows(1)
    kb_ref[BF16_ROWS:hd, :] = zeros_k
    kb_ref[hd:, :] = kt_ref[0, hd:, :]
    k_aug = (ka_ref, kb_ref)
    n_bias = 3
    ones_a = jnp.where((lane >= hd) & (lane < hd + n_bias), 1.0, 0.0).astype(BF16)
    ones_b = jnp.where(lane < n_bias, 1.0, 0.0).astype(BF16)

    zeros_v = jnp.zeros((half, LANES), BF16)
    ones_v = jnp.ones((half, LANES), BF16)
    for jb in range(s_len // half):
        v2 = v_ref[0, jb * half:(jb + 1) * half, :]
        vbd_ref[jb, 0:half, 0:LANES] = jnp.where(is_a, v2, ones_v)
        vbd_ref[jb, 0:half, LANES:] = zeros_v
        vbd_ref[jb, half:, 0:LANES] = zeros_v
        vbd_ref[jb, half:, LANES:] = jnp.where(is_a, ones_v, v2)

    row_t = lax.broadcasted_iota(jnp.int32, (half, half), 0)
    col_t = lax.broadcasted_iota(jnp.int32, (half, half), 1)
    mask_top = col_t <= row_t
    row_b = lax.broadcasted_iota(jnp.int32, (half, tq), 0)
    col_b = lax.broadcasted_iota(jnp.int32, (half, tq), 1)
    mask_bot = col_b <= row_b + half

    def q_body(qi, carry):
        q0 = pl.multiple_of(qi * tq, tq)
        q2 = q_ref[0, pl.ds(q0, tq), :]
        q_heads = (jnp.where(is_a, q2, ones_a), jnp.where(is_a, ones_b, q2))
        m_ref[...] = jnp.full(m_ref.shape, NEG_BIG, F32)
        acc_ref[...] = jnp.zeros(acc_ref.shape, F32)

        def qk(j, dst):
            k0 = pl.multiple_of(j * tq, tq)
            for h in range(2):
                dst[h] = jnp.dot(q_heads[h], k_aug[h][:, pl.ds(k0, tq)], preferred_element_type=F32)

        def softmax_pv(s_heads, r0, jb0, mask):
            nr, tk = s_heads[0].shape
            nblk = tk // half
            ps, alphas = [], []
            for h in range(2):
                s = s_heads[h]
                if mask is not None:
                    s = jnp.where(mask, s, NEG_BIG)
                m_old = m_ref[h, r0:r0 + nr]
                m_new = jnp.maximum(m_old, jnp.max(s, axis=1, keepdims=True))
                alphas.append(jnp.exp2(m_old - m_new))
                ps.append(jnp.exp2((s - pltpu.repeat(m_new, tk // LANES, axis=1)).astype(BF16)))
                m_ref[h, r0:r0 + nr] = m_new
            p_cat = jnp.concatenate([ps[h][:, i * half:(i + 1) * half]
                                     for i in range(nblk) for h in range(2)], axis=1)
            v_blk = vbd_ref[pl.ds(jb0, nblk)].reshape(nblk * 2 * half, 2 * LANES)
            pv = jnp.dot(p_cat, v_blk, preferred_element_type=F32)
            acc_ref[r0:r0 + nr] = jnp.concatenate(alphas, axis=1) * acc_ref[r0:r0 + nr] + pv

        def step(j, src, dst):
            s_heads = (src[0], src[1])
            qk(j + 1, dst)
            softmax_pv(s_heads, 0, 2 * j, None)

        def diagonal(src):
            softmax_pv((src[0, 0:half, 0:half], src[1, 0:half, 0:half]), 0, 2 * qi, mask_top)
            softmax_pv((src[0, half:, :], src[1, half:, :]), half, 2 * qi, mask_bot)

        qk(0, se_ref)

        def pair(t, c):
            step(2 * t, se_ref, so_ref)
            step(2 * t + 1, so_ref, se_ref)
            return c

        lax.fori_loop(0, qi // 2, pair, 0)

        @pl.when(qi % 2 == 0)
        def _():
            diagonal(se_ref)

        @pl.when(qi % 2 == 1)
        def _():
            step(qi - 1, se_ref, so_ref)
            diagonal(so_ref)

        acc_a = acc_ref[:, :LANES]
        acc_b = acc_ref[:, LANES:]
        out = jnp.where(is_a, acc_a / pltpu.roll(acc_a, hd, 1), acc_b / pltpu.roll(acc_b, hd, 1))
        o_ref[0, pl.ds(q0, tq), :] = out.astype(BF16)
        return carry

    lax.fori_loop(0, nq, q_body, 0)


def _attention(q, kt, v, c_pairs, *, hd, tq):
    b, s, d_attn = q.shape
    n_pairs = d_attn // LANES
    blk = lambda bi, p: (bi, 0, p)
    kern = functools.partial(_attn_kernel, tq=tq, hd=hd)
    return pl.pallas_call(
        kern,
        grid=(b, n_pairs),
        in_specs=[
            pl.BlockSpec((1, s, LANES), blk),
            pl.BlockSpec((1, LANES, s), lambda bi, p: (bi, p, 0)),
            pl.BlockSpec((1, s, LANES), blk),
            pl.BlockSpec((1, 1, 2, s), lambda bi, p: (bi, p, 0, 0)),
        ],
        out_specs=pl.BlockSpec((1, s, LANES), blk),
        out_shape=jax.ShapeDtypeStruct((b, s, d_attn), BF16),
        scratch_shapes=[
            pltpu.VMEM((LANES, s), BF16),
            pltpu.VMEM((LANES, s), BF16),
            pltpu.VMEM((2 * s // tq, tq, 2 * LANES), BF16),
            pltpu.VMEM((tq, 2 * LANES), F32),
            pltpu.VMEM((2, tq, LANES), F32),
            pltpu.VMEM((2, tq, tq), F32),
            pltpu.VMEM((2, tq, tq), F32),
        ],
        compiler_params=pltpu.CompilerParams(
            dimension_semantics=("parallel", "parallel"), vmem_limit_bytes=VMEM_LIMIT),
        name="fox_attn",
    )(q, kt, v, c_pairs)


def _ffn_kernel(*refs, has_proj, final_norm, n_chunks):
    refs = list(refs)
    x_ref = refs.pop(0)
    if has_proj:
        ya_ref, yr_ref, wo_ref = refs.pop(0), refs.pop(0), refs.pop(0)
    g_ref, win_ref, wout_ref = refs.pop(0), refs.pop(0), refs.pop(0)
    if final_norm:
        gf_ref = refs.pop(0)
    o_ref = refs.pop(0)

    x = x_ref[...]
    if has_proj:
        y = jnp.concatenate([ya_ref[...], yr_ref[...]], axis=1)
        x = x + jnp.dot(y, wo_ref[...], preferred_element_type=F32)
    hn = _rmsnorm(x, g_ref[...]).astype(BF16)
    d_ff = wout_ref.shape[0]
    fc = d_ff // n_chunks
    acc = None
    for c in range(n_chunks):
        gate = jnp.dot(hn, win_ref[:, c * fc:(c + 1) * fc], preferred_element_type=F32)
        up = jnp.dot(hn, win_ref[:, d_ff + c * fc:d_ff + (c + 1) * fc], preferred_element_type=F32)
        act = (gate * _sigmoid(gate) * up).astype(BF16)
        part = jnp.dot(act, wout_ref[c * fc:(c + 1) * fc, :], preferred_element_type=F32)
        acc = part if acc is None else acc + part
    x = x + 0.5 * acc
    if final_norm:
        x = _rmsnorm(x, gf_ref[...])
    o_ref[...] = x


def _ffn(x, g, w_in, w_out, *, proj=None, final_g=None, tm, n_chunks):
    t, d = x.shape
    const = lambda i: (0, 0)
    rows = lambda i: (i, 0)
    resident = functools.partial(pl.BlockSpec, index_map=const, pipeline_mode=pl.Buffered(1))
    args = [x]
    in_specs = [pl.BlockSpec((tm, d), rows)]
    if proj is not None:
        ya, yr, wo = proj
        args += [ya, yr, wo]
        in_specs += [pl.BlockSpec((tm, ya.shape[1]), rows), pl.BlockSpec((tm, yr.shape[1]), rows),
                     resident(wo.shape)]
    args += [g, w_in, w_out]
    in_specs += [pl.BlockSpec((1, d), const), resident(w_in.shape), resident(w_out.shape)]
    if final_g is not None:
        args.append(final_g)
        in_specs.append(pl.BlockSpec((1, d), const))
    kern = functools.partial(_ffn_kernel, has_proj=proj is not None,
                             final_norm=final_g is not None, n_chunks=n_chunks)
    return pl.pallas_call(
        kern,
        grid=(t // tm,),
        in_specs=in_specs,
        out_specs=pl.BlockSpec((tm, d), rows),
        out_shape=jax.ShapeDtypeStruct((t, d), F32),
        compiler_params=pltpu.CompilerParams(
            dimension_semantics=("parallel",), vmem_limit_bytes=VMEM_LIMIT),
        name="ffn",
    )(*args)


def _pick_tile(n, target):
    t = min(n, target)
    while n % t:
        t //= 2
    return t


def _ffn_chunks(d_ff):
    return 2 if d_ff % (2 * LANES) == 0 else 1


def kernel(x, norm_g, w_in, b_f, conv_w, conv_b, w_rg_a, b_rg_a, w_rg_x, b_rg_x, rg_lambda,
           w_out, w_ffn_in, w_ffn_out, final_g):
    b, s, d = x.shape
    depth = norm_g.shape[0]
    n_heads = b_f.shape[-1]
    d_rec = conv_w.shape[-1]
    n_blocks = w_rg_a.shape[1]
    d_attn = (w_in.shape[-1] - n_heads - 2 * d_rec) // 3
    hd = d_attn // n_heads
    d_ff = w_ffn_out.shape[2]
    assert 2 * hd == LANES and n_heads % 2 == 0 and n_heads <= SUBLANES
    assert d_attn % LANES == 0 and d_rec % LANES == 0 and d % LANES == 0

    tm_mix = _pick_tile(s, 512)
    tq = _pick_tile(s, 512)
    tm_ffn = _pick_tile(b * s, 512)
    n_chunks = _ffn_chunks(d_ff)

    o_k, o_v, o_f = d_attn, 2 * d_attn, 3 * d_attn
    o_xr = o_f + n_heads
    w_rows = jnp.concatenate([w_in[:, :, :o_k], w_in[:, :, o_v:o_f], w_in[:, :, o_xr:]],
                             axis=-1).astype(BF16)
    w_f = jnp.pad(w_in[:, :, o_f:o_xr], ((0, 0), (0, 0), (0, BF16_ROWS - n_heads)))
    w_cols = jnp.swapaxes(jnp.concatenate([w_in[:, :, o_k:o_v], w_f], axis=-1), 1, 2).astype(BF16)
    bf_rep = jnp.broadcast_to(
        jnp.pad(b_f, ((0, 0), (0, SUBLANES - n_heads)))[:, :, None], (depth, SUBLANES, LANES))
    tri = jnp.concatenate([jnp.triu(jnp.ones((tm_mix, tm_mix), BF16)),
                           jnp.ones((tm_mix, LANES), BF16)], axis=1)
    eye = jnp.eye(n_blocks, dtype=F32)[None, :, None, :, None]

    def block_diag(w):
        return (w[:, :, :, None, :] * eye).reshape(depth, d_rec, d_rec).astype(BF16)

    wa_bd = block_diag(w_rg_a)
    wx_bd = block_diag(w_rg_x)
    w_out_b = w_out.astype(BF16)
    w_ffn_in_b = w_ffn_in.astype(BF16)
    w_ffn_out_b = w_ffn_out.astype(BF16)
    row = lambda a: a[:, None, :]
    conv_b2, ba2, bx2, lam2 = row(conv_b), row(b_rg_a), row(b_rg_x), row(rg_lambda)

    xf = x.reshape(b * s, d)
    for l in range(depth):
        g = norm_g[l][:, None, :]
        xf = _ffn(xf, g[0], w_ffn_in_b[l, 0], w_ffn_out_b[l, 0], tm=tm_ffn, n_chunks=n_chunks)
        q, kt, v, ct, y_rec = _mixer_in(
            xf.reshape(b, s, d), g[1], w_rows[l], w_cols[l], bf_rep[l], tri, conv_w[l], conv_b2[l],
            wa_bd[l], ba2[l], wx_bd[l], bx2[l], lam2[l],
            d_attn=d_attn, d_rec=d_rec, scale=LOG2E / math.sqrt(hd), tm=tm_mix)
        c_pairs = ct[:, :n_heads, :].reshape(b, n_heads // 2, 2, s)
        y_attn = _attention(q, kt, v, c_pairs, hd=hd, tq=tq)
        last = l == depth - 1
        xf = _ffn(xf, g[2], w_ffn_in_b[l, 1], w_ffn_out_b[l, 1],
                  proj=(y_attn.reshape(b * s, d_attn), y_rec.reshape(b * s, d_rec), w_out_b[l]),
                  final_g=final_g[None, :] if last else None, tm=tm_ffn, n_chunks=n_chunks)
    return xf.reshape(b, s, d)
```

```python
import functools
import math

import jax
import jax.numpy as jnp
from jax import lax
from jax.experimental import pallas as pl
from jax.experimental.pallas import tpu as pltpu

EPS = 1e-6
RG_C = 8.0
LANES = 128
SUBLANES = 8
BF16_ROWS = 16
MXU_TILE = 256
NEG_BIG = -1e30
LOG2E = math.log2(math.e)
VMEM_LIMIT = 52 * 1024 * 1024

F32 = jnp.float32
BF16 = jnp.bfloat16
NT_DIMS = (((1,), (1,)), ((), ()))


def _rmsnorm(x, g):
    ms = jnp.mean(x * x, axis=-1, keepdims=True)
    return x * lax.rsqrt(ms + EPS) * g


def _sigmoid(x):
    return 1.0 / (1.0 + jnp.exp(-x))


def _softplus(x):
    return jnp.maximum(x, 0.0) + jnp.log1p(jnp.exp(-jnp.abs(x)))


def _gelu_tanh(x):
    c = math.sqrt(2.0 / math.pi)
    return (0.5 * x) * (1.0 + jnp.tanh(x * (c + (c * 0.044715) * (x * x))))


def _shift_rows(x, k, fill):
    rows = lax.broadcasted_iota(jnp.int32, x.shape, 0)
    return jnp.where(rows >= k, pltpu.roll(x, k, 0), fill)


def _tile_lanes(x, n):
    return jnp.concatenate([x] * n, axis=1)


def _split3(x):
    hi = x.astype(BF16).astype(F32)
    r = x - hi
    mid = r.astype(BF16).astype(F32)
    lo = (r - mid).astype(BF16).astype(F32)
    return hi, mid, lo


def _mixer_in_kernel(x_ref, g_ref, w_ref, wt_ref, bf_ref, tri_ref, cw_ref, cb_ref, wa_ref, ba_ref,
                     wx_ref, bx_ref, lam_ref,
                     q_ref, kt_ref, v_ref, ct_ref, yr_ref,
                     c_carry, h_carry, x_carry, a_s, u_s, hs_s, *, d_attn, d_rec, scale):
    j = pl.program_id(1)
    tm = x_ref.shape[1]
    nb = tm // SUBLANES

    @pl.when(j == 0)
    def _():
        c_carry[...] = jnp.zeros_like(c_carry)
        h_carry[...] = jnp.zeros_like(h_carry)
        x_carry[...] = jnp.zeros_like(x_carry)

    hn = _rmsnorm(x_ref[0], g_ref[...]).astype(BF16)
    z = jnp.dot(hn, w_ref[...], preferred_element_type=F32)
    zt = lax.dot_general(wt_ref[...], hn, NT_DIMS, preferred_element_type=F32)

    q_ref[0] = (z[:, :d_attn] * scale).astype(BF16)
    v_ref[0] = z[:, d_attn:2 * d_attn].astype(BF16)
    kt_ref[0] = zt[:d_attn].astype(BF16)
    o_xr = 2 * d_attn
    o_gr = o_xr + d_rec
    xr = z[:, o_xr:o_gr]
    gr = z[:, o_gr:]

    f_logit = zt[d_attn:d_attn + SUBLANES] + _tile_lanes(bf_ref[...], tm // LANES)
    log_f = -_softplus(-f_logit)
    parts = jnp.concatenate(_split3(log_f) + (jnp.zeros_like(log_f),), axis=0).astype(BF16)
    cs = jnp.dot(parts, tri_ref[...], preferred_element_type=F32)
    cs = cs[0:SUBLANES] + cs[SUBLANES:2 * SUBLANES] + cs[2 * SUBLANES:3 * SUBLANES]
    carry = c_carry[...]
    ct_ref[0] = cs[:, :tm] + _tile_lanes(carry, tm // LANES)
    c_carry[...] = carry + cs[:, tm:]

    width = cw_ref.shape[0]
    ext = jnp.concatenate([x_carry[...], xr], axis=0)
    x_carry[...] = xr[tm - SUBLANES:, :]
    xc = xr * cw_ref[width - 1:width, :] + cb_ref[...]
    for s in range(1, width):
        xc = xc + pltpu.roll(ext, s, 0)[SUBLANES:, :] * cw_ref[width - 1 - s:width - s, :]

    xcb = xc.astype(BF16)
    r = _sigmoid(jnp.dot(xcb, wa_ref[...], preferred_element_type=F32) + ba_ref[...])
    i = _sigmoid(jnp.dot(xcb, wx_ref[...], preferred_element_type=F32) + bx_ref[...])
    log_a = (-RG_C * _softplus(-lam_ref[...])) * r
    a = jnp.exp(log_a)
    u = jnp.sqrt(-jnp.tanh(log_a) * (1.0 + a * a)) * (i * xc)

    a3 = a.reshape(nb, SUBLANES, d_rec)
    u3 = u.reshape(nb, SUBLANES, d_rec)
    sub = lax.broadcasted_iota(jnp.int32, a3.shape, 1)
    k = 1
    while k < SUBLANES:
        keep = sub >= k
        u3 = u3 + a3 * jnp.where(keep, pltpu.roll(u3, k, 1), 0.0)
        a3 = a3 * jnp.where(keep, pltpu.roll(a3, k, 1), 1.0)
        k *= 2
    a2 = a3.reshape(tm, d_rec)
    u2 = u3.reshape(tm, d_rec)
    n_lt = d_rec // LANES
    for c in range(n_lt):
        a_s[c] = a2[:, c * LANES:(c + 1) * LANES]
        u_s[c] = u2[:, c * LANES:(c + 1) * LANES]
    last_rows = pl.ds(SUBLANES - 1, nb, stride=SUBLANES)
    ab = jnp.concatenate([a_s[c, last_rows, :] for c in range(n_lt)], axis=1)
    ub = jnp.concatenate([u_s[c, last_rows, :] for c in range(n_lt)], axis=1)
    k = 1
    while k < nb:
        ub = ub + ab * _shift_rows(ub, k, 0.0)
        ab = ab * _shift_rows(ab, k, 1.0)
        k *= 2
    h_in = h_carry[...]
    h_end = ab * h_in + ub
    h_carry[...] = h_end[nb - 1:nb, :]
    hs_s[...] = _shift_rows(h_end, 1, h_in)
    for b in range(nb):
        rows = slice(b * SUBLANES, (b + 1) * SUBLANES)
        for c in range(n_lt):
            u_s[c, rows, :] = (a_s[c, rows, :] * hs_s[b:b + 1, c * LANES:(c + 1) * LANES]
                               + u_s[c, rows, :])
    h = jnp.concatenate([u_s[c] for c in range(n_lt)], axis=1)

    yr_ref[0] = (h * _gelu_tanh(gr)).astype(BF16)


def _mixer_in(x, g, w, wt, bf_rep, tri, conv_w, conv_b, wa, ba, wx, bx, lam, *, d_attn, d_rec, scale, tm):
    b, s, d = x.shape
    const = lambda *_: (0, 0)
    row3 = lambda bi, j: (bi, j, 0)
    col3 = lambda bi, j: (bi, 0, j)
    kern = functools.partial(_mixer_in_kernel, d_attn=d_attn, d_rec=d_rec, scale=scale)
    return pl.pallas_call(
        kern,
        grid=(b, s // tm),
        in_specs=[
            pl.BlockSpec((1, tm, d), row3),
            pl.BlockSpec((1, d), const),
            pl.BlockSpec(w.shape, const),
            pl.BlockSpec(wt.shape, const),
            pl.BlockSpec((SUBLANES, LANES), const),
            pl.BlockSpec(tri.shape, const),
            pl.BlockSpec(conv_w.shape, const),
            pl.BlockSpec((1, d_rec), const),
            pl.BlockSpec((d_rec, d_rec), const),
            pl.BlockSpec((1, d_rec), const),
            pl.BlockSpec((d_rec, d_rec), const),
            pl.BlockSpec((1, d_rec), const),
            pl.BlockSpec((1, d_rec), const),
        ],
        out_specs=[
            pl.BlockSpec((1, tm, d_attn), row3),
            pl.BlockSpec((1, d_attn, tm), col3),
            pl.BlockSpec((1, tm, d_attn), row3),
            pl.BlockSpec((1, SUBLANES, tm), col3),
            pl.BlockSpec((1, tm, d_rec), row3),
        ],
        out_shape=[
            jax.ShapeDtypeStruct((b, s, d_attn), BF16),
            jax.ShapeDtypeStruct((b, d_attn, s), BF16),
            jax.ShapeDtypeStruct((b, s, d_attn), BF16),
            jax.ShapeDtypeStruct((b, SUBLANES, s), F32),
            jax.ShapeDtypeStruct((b, s, d_rec), BF16),
        ],
        scratch_shapes=[
            pltpu.VMEM((SUBLANES, LANES), F32),
            pltpu.VMEM((1, d_rec), F32),
            pltpu.VMEM((SUBLANES, d_rec), F32),
            pltpu.VMEM((d_rec // LANES, tm, LANES), F32),
            pltpu.VMEM((d_rec // LANES, tm, LANES), F32),
            pltpu.VMEM((tm // SUBLANES, d_rec), F32),
        ],
        compiler_params=pltpu.CompilerParams(
            dimension_semantics=("parallel", "arbitrary"), vmem_limit_bytes=VMEM_LIMIT),
        name="mixer_in",
    )(x, g, w, wt, bf_rep, tri, conv_w, conv_b, wa, ba, wx, bx, lam)


def _attn_kernel(q_ref, kt_ref, v_ref, c_ref, o_ref, ka_ref, kb_ref, vbd_ref, acc_ref, m_ref,
                 se_ref, so_ref, *, tq, hd):
    s_len = q_ref.shape[1]
    nq = s_len // tq
    half = tq // 2
    lane = lax.broadcasted_iota(jnp.int32, (1, LANES), 1)
    is_a = lane < hd

    sub = lax.broadcasted_iota(jnp.int32, (BF16_ROWS, s_len), 0)

    def bias_rows(h):
        hi, mid, lo = _split3(c_ref[0, 0, h:h + 1, :] * (-LOG2E))
        rows = jnp.where(sub == 0, hi, jnp.where(sub == 1, mid, jnp.where(sub == 2, lo, 0.0)))
        return rows.astype(BF16)

    zeros_k = jnp.zeros((hd - BF16_ROWS, s_len), BF16)
    ka_ref[0:hd, :] = kt_ref[0, 0:hd, :]
    ka_ref[hd:hd + BF16_ROWS, :] = bias_rows(0)
    ka_ref[hd + BF16_ROWS:, :] = zeros_k
    kb_ref[0:BF16_ROWS, :] = bias_rows(1)
    kb_ref[BF16_ROWS:hd, :] = zeros_k
    kb_ref[hd:, :] = kt_ref[0, hd:, :]
    k_aug = (ka_ref, kb_ref)
    n_bias = 3
    ones_a = jnp.where((lane >= hd) & (lane < hd + n_bias), 1.0, 0.0).astype(BF16)
    ones_b = jnp.where(lane < n_bias, 1.0, 0.0).astype(BF16)

    zeros_v = jnp.zeros((half, LANES), BF16)
    ones_v = jnp.ones((half, LANES), BF16)
    for jb in range(s_len // half):
        v2 = v_ref[0, jb * half:(jb + 1) * half, :]
        vbd_ref[jb, 0:half, 0:LANES] = jnp.where(is_a, v2, ones_v)
        vbd_ref[jb, 0:half, LANES:] = zeros_v
        vbd_ref[jb, half:, 0:LANES] = zeros_v
        vbd_ref[jb, half:, LANES:] = jnp.where(is_a, ones_v, v2)

    row_t = lax.broadcasted_iota(jnp.int32, (half, half), 0)
    col_t = lax.broadcasted_iota(jnp.int32, (half, half), 1)
    mask_top = col_t <= row_t
    row_b = lax.broadcasted_iota(jnp.int32, (half, tq), 0)
    col_b = lax.broadcasted_iota(jnp.int32, (half, tq), 1)
    mask_bot = col_b <= row_b + half

    def q_body(qi, carry):
        q0 = pl.multiple_of(qi * tq, tq)
        q2 = q_ref[0, pl.ds(q0, tq), :]
        q_heads = (jnp.where(is_a, q2, ones_a), jnp.where(is_a, ones_b, q2))
        m_ref[...] = jnp.full(m_ref.shape, NEG_BIG, F32)
        acc_ref[...] = jnp.zeros(acc_ref.shape, F32)

        def qk(j, dst):
            k0 = pl.multiple_of(j * tq, tq)
            for h in range(2):
                dst[h] = jnp.dot(q_heads[h], k_aug[h][:, pl.ds(k0, tq)], preferred_element_type=F32)

        def softmax_pv(s_heads, r0, jb0, mask):
            nr, tk = s_heads[0].shape
            nblk = tk // half
            ps, alphas = [], []
            for h in range(2):
                s = s_heads[h]
                if mask is not None:
                    s = jnp.where(mask, s, NEG_BIG)
                m_old = m_ref[h, r0:r0 + nr]
                m_new = jnp.maximum(m_old, jnp.max(s, axis=1, keepdims=True))
                alphas.append(jnp.exp2(m_old - m_new))
                ps.append(jnp.exp2((s - _tile_lanes(m_new, tk // LANES)).astype(BF16)))
                m_ref[h, r0:r0 + nr] = m_new
            p_cat = jnp.concatenate([ps[h][:, i * half:(i + 1) * half]
                                     for i in range(nblk) for h in range(2)], axis=1)
            v_blk = vbd_ref[pl.ds(jb0, nblk)].reshape(nblk * 2 * half, 2 * LANES)
            pv = jnp.dot(p_cat, v_blk, preferred_element_type=F32)
            acc_ref[r0:r0 + nr] = jnp.concatenate(alphas, axis=1) * acc_ref[r0:r0 + nr] + pv

        def step(j, src, dst):
            s_heads = (src[0], src[1])
            qk(j + 1, dst)
            softmax_pv(s_heads, 0, 2 * j, None)

        def diagonal(src):
            softmax_pv((src[0, 0:half, 0:half], src[1, 0:half, 0:half]), 0, 2 * qi, mask_top)
            softmax_pv((src[0, half:, :], src[1, half:, :]), half, 2 * qi, mask_bot)

        qk(0, se_ref)

        def pair(t, c):
            step(2 * t, se_ref, so_ref)
            step(2 * t + 1, so_ref, se_ref)
            return c

        lax.fori_loop(0, qi // 2, pair, 0)

        @pl.when(qi % 2 == 0)
        def _():
            diagonal(se_ref)

        @pl.when(qi % 2 == 1)
        def _():
            step(qi - 1, se_ref, so_ref)
            diagonal(so_ref)

        acc_a = acc_ref[:, :LANES]
        acc_b = acc_ref[:, LANES:]
        out = jnp.where(is_a, acc_a / pltpu.roll(acc_a, hd, 1), acc_b / pltpu.roll(acc_b, hd, 1))
        o_ref[0, pl.ds(q0, tq), :] = out.astype(BF16)
        return carry

    lax.fori_loop(0, nq, q_body, 0)


def _attention(q, kt, v, c_pairs, *, hd, tq):
    b, s, d_attn = q.shape
    n_pairs = d_attn // LANES
    blk = lambda bi, p: (bi, 0, p)
    kern = functools.partial(_attn_kernel, tq=tq, hd=hd)
    return pl.pallas_call(
        kern,
        grid=(b, n_pairs),
        in_specs=[
            pl.BlockSpec((1, s, LANES), blk),
            pl.BlockSpec((1, LANES, s), lambda bi, p: (bi, p, 0)),
            pl.BlockSpec((1, s, LANES), blk),
            pl.BlockSpec((1, 1, 2, s), lambda bi, p: (bi, p, 0, 0)),
        ],
        out_specs=pl.BlockSpec((1, s, LANES), blk),
        out_shape=jax.ShapeDtypeStruct((b, s, d_attn), BF16),
        scratch_shapes=[
            pltpu.VMEM((LANES, s), BF16),
            pltpu.VMEM((LANES, s), BF16),
            pltpu.VMEM((2 * s // tq, tq, 2 * LANES), BF16),
            pltpu.VMEM((tq, 2 * LANES), F32),
            pltpu.VMEM((2, tq, LANES), F32),
            pltpu.VMEM((2, tq, tq), F32),
            pltpu.VMEM((2, tq, tq), F32),
        ],
        compiler_params=pltpu.CompilerParams(
            dimension_semantics=("parallel", "parallel"), vmem_limit_bytes=VMEM_LIMIT),
        name="fox_attn",
    )(q, kt, v, c_pairs)


def _ffn_kernel(*refs, has_proj, final_norm, chunks):
    refs = list(refs)
    x_ref = refs.pop(0)
    if has_proj:
        ya_ref, yr_ref, wo_ref = refs.pop(0), refs.pop(0), refs.pop(0)
    g_ref, win_ref, wout_ref = refs.pop(0), refs.pop(0), refs.pop(0)
    if final_norm:
        gf_ref = refs.pop(0)
    o_ref = refs.pop(0)

    x = x_ref[...]
    if has_proj:
        y = jnp.concatenate([ya_ref[...], yr_ref[...]], axis=1)
        x = x + jnp.dot(y, wo_ref[...], preferred_element_type=F32)
    hn = _rmsnorm(x, g_ref[...]).astype(BF16)
    d_ff = wout_ref.shape[0]
    acc = None
    for lo, hi in zip(chunks[:-1], chunks[1:]):
        gate = jnp.dot(hn, win_ref[:, lo:hi], preferred_element_type=F32)
        up = jnp.dot(hn, win_ref[:, d_ff + lo:d_ff + hi], preferred_element_type=F32)
        act = (gate * _sigmoid(gate) * up).astype(BF16)
        part = jnp.dot(act, wout_ref[lo:hi, :], preferred_element_type=F32)
        acc = part if acc is None else acc + part
    x = x + 0.5 * acc
    if final_norm:
        x = _rmsnorm(x, gf_ref[...])
    o_ref[...] = x


def _ffn(x, g, w_in, w_out, *, proj=None, final_g=None, tm, chunks):
    t, d = x.shape
    const = lambda i: (0, 0)
    rows = lambda i: (i, 0)
    resident = functools.partial(pl.BlockSpec, index_map=const, pipeline_mode=pl.Buffered(1))
    args = [x]
    in_specs = [pl.BlockSpec((tm, d), rows)]
    if proj is not None:
        ya, yr, wo = proj
        args += [ya, yr, wo]
        in_specs += [pl.BlockSpec((tm, ya.shape[1]), rows), pl.BlockSpec((tm, yr.shape[1]), rows),
                     resident(wo.shape)]
    args += [g, w_in, w_out]
    in_specs += [pl.BlockSpec((1, d), const), resident(w_in.shape), resident(w_out.shape)]
    if final_g is not None:
        args.append(final_g)
        in_specs.append(pl.BlockSpec((1, d), const))
    kern = functools.partial(_ffn_kernel, has_proj=proj is not None,
                             final_norm=final_g is not None, chunks=chunks)
    return pl.pallas_call(
        kern,
        grid=(t // tm,),
        in_specs=in_specs,
        out_specs=pl.BlockSpec((tm, d), rows),
        out_shape=jax.ShapeDtypeStruct((t, d), F32),
        compiler_params=pltpu.CompilerParams(
            dimension_semantics=("parallel",), vmem_limit_bytes=VMEM_LIMIT),
        name="ffn",
    )(*args)


def _pick_tile(n, target):
    t = min(n, target)
    while n % t:
        t //= 2
    return t


def _ffn_chunks(d_ff):
    if d_ff % MXU_TILE or d_ff < 2 * MXU_TILE:
        return (0, d_ff)
    tiles = d_ff // MXU_TILE
    return (0, (tiles + 1) // 2 * MXU_TILE, d_ff)


def kernel(x, norm_g, w_in, b_f, conv_w, conv_b, w_rg_a, b_rg_a, w_rg_x, b_rg_x, rg_lambda,
           w_out, w_ffn_in, w_ffn_out, final_g):
    b, s, d = x.shape
    depth = norm_g.shape[0]
    n_heads = b_f.shape[-1]
    d_rec = conv_w.shape[-1]
    n_blocks = w_rg_a.shape[1]
    d_attn = (w_in.shape[-1] - n_heads - 2 * d_rec) // 3
    hd = d_attn // n_heads
    d_ff = w_ffn_out.shape[2]
    assert 2 * hd == LANES and n_heads % 2 == 0 and n_heads <= SUBLANES
    assert d_attn % LANES == 0 and d_rec % LANES == 0 and d % LANES == 0

    tm_mix = _pick_tile(s, 512)
    tq = _pick_tile(s, 512)
    tm_ffn = _pick_tile(b * s, 512)
    chunks = _ffn_chunks(d_ff)

    o_k, o_v, o_f = d_attn, 2 * d_attn, 3 * d_attn
    o_xr = o_f + n_heads
    w_rows = jnp.concatenate([w_in[:, :, :o_k], w_in[:, :, o_v:o_f], w_in[:, :, o_xr:]],
                             axis=-1).astype(BF16)
    w_f = jnp.pad(w_in[:, :, o_f:o_xr], ((0, 0), (0, 0), (0, BF16_ROWS - n_heads)))
    w_cols = jnp.swapaxes(jnp.concatenate([w_in[:, :, o_k:o_v], w_f], axis=-1), 1, 2).astype(BF16)
    bf_rep = jnp.broadcast_to(
        jnp.pad(b_f, ((0, 0), (0, SUBLANES - n_heads)))[:, :, None], (depth, SUBLANES, LANES))
    tri = jnp.concatenate([jnp.triu(jnp.ones((tm_mix, tm_mix), BF16)),
                           jnp.ones((tm_mix, LANES), BF16)], axis=1)
    eye = jnp.eye(n_blocks, dtype=F32)[None, :, None, :, None]

    def block_diag(w):
        return (w[:, :, :, None, :] * eye).reshape(depth, d_rec, d_rec).astype(BF16)

    wa_bd = block_diag(w_rg_a)
    wx_bd = block_diag(w_rg_x)
    w_out_b = w_out.astype(BF16)
    w_ffn_in_b = w_ffn_in.astype(BF16)
    w_ffn_out_b = w_ffn_out.astype(BF16)
    row = lambda a: a[:, None, :]
    conv_b2, ba2, bx2, lam2 = row(conv_b), row(b_rg_a), row(b_rg_x), row(rg_lambda)

    xf = x.reshape(b * s, d)
    for l in range(depth):
        g = norm_g[l][:, None, :]
        xf = _ffn(xf, g[0], w_ffn_in_b[l, 0], w_ffn_out_b[l, 0], tm=tm_ffn, chunks=chunks)
        q, kt, v, ct, y_rec = _mixer_in(
            xf.reshape(b, s, d), g[1], w_rows[l], w_cols[l], bf_rep[l], tri, conv_w[l], conv_b2[l],
            wa_bd[l], ba2[l], wx_bd[l], bx2[l], lam2[l],
            d_attn=d_attn, d_rec=d_rec, scale=LOG2E / math.sqrt(hd), tm=tm_mix)
        c_pairs = ct[:, :n_heads, :].reshape(b, n_heads // 2, 2, s)
        y_attn = _attention(q, kt, v, c_pairs, hd=hd, tq=tq)
        last = l == depth - 1
        xf = _ffn(xf, g[2], w_ffn_in_b[l, 1], w_ffn_out_b[l, 1],
                  proj=(y_attn.reshape(b * s, d_attn), y_rec.reshape(b * s, d_rec), w_out_b[l]),
                  final_g=final_g[None, :] if last else None, tm=tm_ffn, chunks=chunks)
    return xf.reshape(b, s, d)
```

```python
import functools
import math

import jax
import jax.numpy as jnp
from jax import lax
from jax.experimental import pallas as pl
from jax.experimental.pallas import tpu as pltpu

EPS = 1e-6
RG_C = 8.0
LANES = 128
SUBLANES = 8
BF16_ROWS = 16
MXU_TILE = 256
NEG_BIG = -1e30
LOG2E = math.log2(math.e)
VMEM_LIMIT = 56 * 1024 * 1024

F32 = jnp.float32
BF16 = jnp.bfloat16
NT_DIMS = (((1,), (1,)), ((), ()))


def _rmsnorm(x, g):
    ms = jnp.mean(x * x, axis=-1, keepdims=True)
    return x * lax.rsqrt(ms + EPS) * g


def _sigmoid(x):
    return 1.0 / (1.0 + jnp.exp(-x))


def _softplus(x):
    return jnp.maximum(x, 0.0) + jnp.log1p(jnp.exp(-jnp.abs(x)))


def _gelu_tanh(x):
    c = math.sqrt(2.0 / math.pi)
    return (0.5 * x) * (1.0 + jnp.tanh(x * (c + (c * 0.044715) * (x * x))))


def _shift_rows(x, k, fill):
    rows = lax.broadcasted_iota(jnp.int32, x.shape, 0)
    return jnp.where(rows >= k, pltpu.roll(x, k, 0), fill)


def _tile_lanes(x, n):
    return jnp.concatenate([x] * n, axis=1)


def _split3(x):
    hi = x.astype(BF16).astype(F32)
    r = x - hi
    mid = r.astype(BF16).astype(F32)
    lo = (r - mid).astype(BF16).astype(F32)
    return hi, mid, lo


def _mixer_in_kernel(x_ref, g_ref, w_ref, wt_ref, bf_ref, tri_ref, cw_ref, cb_ref, wa_ref, ba_ref,
                     wx_ref, bx_ref, lam_ref,
                     q_ref, kt_ref, v_ref, ct_ref, yr_ref,
                     c_carry, h_carry, x_carry, a_s, u_s, hs_s, *, d_attn, d_rec, scale):
    j = pl.program_id(1)
    tm = x_ref.shape[1]
    nb = tm // SUBLANES

    @pl.when(j == 0)
    def _():
        c_carry[...] = jnp.zeros_like(c_carry)
        h_carry[...] = jnp.zeros_like(h_carry)
        x_carry[...] = jnp.zeros_like(x_carry)

    hn = _rmsnorm(x_ref[0], g_ref[...]).astype(BF16)
    z = jnp.dot(hn, w_ref[...], preferred_element_type=F32)
    zt = lax.dot_general(wt_ref[...], hn, NT_DIMS, preferred_element_type=F32)

    q_ref[0] = (z[:, :d_attn] * scale).astype(BF16)
    v_ref[0] = z[:, d_attn:2 * d_attn].astype(BF16)
    kt_ref[0] = zt[:d_attn].astype(BF16)
    o_xr = 2 * d_attn
    o_gr = o_xr + d_rec
    xr = z[:, o_xr:o_gr]
    gr = z[:, o_gr:]

    f_logit = zt[d_attn:d_attn + SUBLANES] + _tile_lanes(bf_ref[...], tm // LANES)
    log_f = -_softplus(-f_logit)
    parts = jnp.concatenate(_split3(log_f) + (jnp.zeros_like(log_f),), axis=0).astype(BF16)
    cs = jnp.dot(parts, tri_ref[...], preferred_element_type=F32)
    cs = cs[0:SUBLANES] + cs[SUBLANES:2 * SUBLANES] + cs[2 * SUBLANES:3 * SUBLANES]
    carry = c_carry[...]
    ct_ref[0] = cs[:, :tm] + _tile_lanes(carry, tm // LANES)
    c_carry[...] = carry + cs[:, tm:]

    width = cw_ref.shape[0]
    ext = jnp.concatenate([x_carry[...], xr], axis=0)
    x_carry[...] = xr[tm - SUBLANES:, :]
    xc = xr * cw_ref[width - 1:width, :] + cb_ref[...]
    for s in range(1, width):
        xc = xc + pltpu.roll(ext, s, 0)[SUBLANES:, :] * cw_ref[width - 1 - s:width - s, :]

    xcb = xc.astype(BF16)
    r = _sigmoid(jnp.dot(xcb, wa_ref[...], preferred_element_type=F32) + ba_ref[...])
    i = _sigmoid(jnp.dot(xcb, wx_ref[...], preferred_element_type=F32) + bx_ref[...])
    log_a = (-RG_C * _softplus(-lam_ref[...])) * r
    a = jnp.exp(log_a)
    u = jnp.sqrt(-jnp.tanh(log_a) * (1.0 + a * a)) * (i * xc)

    a3 = a.reshape(nb, SUBLANES, d_rec)
    u3 = u.reshape(nb, SUBLANES, d_rec)
    sub = lax.broadcasted_iota(jnp.int32, a3.shape, 1)
    k = 1
    while k < SUBLANES:
        keep = sub >= k
        u3 = u3 + a3 * jnp.where(keep, pltpu.roll(u3, k, 1), 0.0)
        a3 = a3 * jnp.where(keep, pltpu.roll(a3, k, 1), 1.0)
        k *= 2
    a2 = a3.reshape(tm, d_rec)
    u2 = u3.reshape(tm, d_rec)
    n_lt = d_rec // LANES
    for c in range(n_lt):
        a_s[c] = a2[:, c * LANES:(c + 1) * LANES]
        u_s[c] = u2[:, c * LANES:(c + 1) * LANES]
    last_rows = pl.ds(SUBLANES - 1, nb, stride=SUBLANES)
    ab = jnp.concatenate([a_s[c, last_rows, :] for c in range(n_lt)], axis=1)
    ub = jnp.concatenate([u_s[c, last_rows, :] for c in range(n_lt)], axis=1)
    k = 1
    while k < nb:
        ub = ub + ab * _shift_rows(ub, k, 0.0)
        ab = ab * _shift_rows(ab, k, 1.0)
        k *= 2
    h_in = h_carry[...]
    h_end = ab * h_in + ub
    h_carry[...] = h_end[nb - 1:nb, :]
    hs_s[...] = _shift_rows(h_end, 1, h_in)
    for b in range(nb):
        rows = slice(b * SUBLANES, (b + 1) * SUBLANES)
        for c in range(n_lt):
            u_s[c, rows, :] = (a_s[c, rows, :] * hs_s[b:b + 1, c * LANES:(c + 1) * LANES]
                               + u_s[c, rows, :])
    h = jnp.concatenate([u_s[c] for c in range(n_lt)], axis=1)

    yr_ref[0] = (h * _gelu_tanh(gr)).astype(BF16)


def _mixer_in(x, g, w, wt, bf_rep, tri, conv_w, conv_b, wa, ba, wx, bx, lam, *, d_attn, d_rec, scale, tm):
    b, s, d = x.shape
    const = lambda *_: (0, 0)
    row3 = lambda bi, j: (bi, j, 0)
    col3 = lambda bi, j: (bi, 0, j)
    kern = functools.partial(_mixer_in_kernel, d_attn=d_attn, d_rec=d_rec, scale=scale)
    return pl.pallas_call(
        kern,
        grid=(b, s // tm),
        in_specs=[
            pl.BlockSpec((1, tm, d), row3),
            pl.BlockSpec((1, d), const),
            pl.BlockSpec(w.shape, const),
            pl.BlockSpec(wt.shape, const),
            pl.BlockSpec((SUBLANES, LANES), const),
            pl.BlockSpec(tri.shape, const),
            pl.BlockSpec(conv_w.shape, const),
            pl.BlockSpec((1, d_rec), const),
            pl.BlockSpec((d_rec, d_rec), const),
            pl.BlockSpec((1, d_rec), const),
            pl.BlockSpec((d_rec, d_rec), const),
            pl.BlockSpec((1, d_rec), const),
            pl.BlockSpec((1, d_rec), const),
        ],
        out_specs=[
            pl.BlockSpec((1, tm, d_attn), row3),
            pl.BlockSpec((1, d_attn, tm), col3),
            pl.BlockSpec((1, tm, d_attn), row3),
            pl.BlockSpec((1, SUBLANES, tm), col3),
            pl.BlockSpec((1, tm, d_rec), row3),
        ],
        out_shape=[
            jax.ShapeDtypeStruct((b, s, d_attn), BF16),
            jax.ShapeDtypeStruct((b, d_attn, s), BF16),
            jax.ShapeDtypeStruct((b, s, d_attn), BF16),
            jax.ShapeDtypeStruct((b, SUBLANES, s), F32),
            jax.ShapeDtypeStruct((b, s, d_rec), BF16),
        ],
        scratch_shapes=[
            pltpu.VMEM((SUBLANES, LANES), F32),
            pltpu.VMEM((1, d_rec), F32),
            pltpu.VMEM((SUBLANES, d_rec), F32),
            pltpu.VMEM((d_rec // LANES, tm, LANES), F32),
            pltpu.VMEM((d_rec // LANES, tm, LANES), F32),
            pltpu.VMEM((tm // SUBLANES, d_rec), F32),
        ],
        compiler_params=pltpu.CompilerParams(
            dimension_semantics=("parallel", "arbitrary"), vmem_limit_bytes=VMEM_LIMIT),
        name="mixer_in",
    )(x, g, w, wt, bf_rep, tri, conv_w, conv_b, wa, ba, wx, bx, lam)


def _attn_kernel(q_ref, kt_ref, v_ref, c_ref, o_ref, ka_ref, kb_ref, vbd_ref, acc_ref, m_ref,
                 se_ref, so_ref, *, tq, hd):
    s_len = q_ref.shape[1]
    nq = s_len // tq
    half = tq // 2
    lane = lax.broadcasted_iota(jnp.int32, (1, LANES), 1)
    is_a = lane < hd

    sub = lax.broadcasted_iota(jnp.int32, (BF16_ROWS, s_len), 0)

    def bias_rows(h):
        hi, mid, lo = _split3(c_ref[0, 0, h:h + 1, :] * (-LOG2E))
        rows = jnp.where(sub == 0, hi, jnp.where(sub == 1, mid, jnp.where(sub == 2, lo, 0.0)))
        return rows.astype(BF16)

    zeros_k = jnp.zeros((hd - BF16_ROWS, s_len), BF16)
    ka_ref[0:hd, :] = kt_ref[0, 0:hd, :]
    ka_ref[hd:hd + BF16_ROWS, :] = bias_rows(0)
    ka_ref[hd + BF16_ROWS:, :] = zeros_k
    kb_ref[0:BF16_ROWS, :] = bias_rows(1)
    kb_ref[BF16_ROWS:hd, :] = zeros_k
    kb_ref[hd:, :] = kt_ref[0, hd:, :]
    k_aug = (ka_ref, kb_ref)
    n_bias = 3
    ones_a = jnp.where((lane >= hd) & (lane < hd + n_bias), 1.0, 0.0).astype(BF16)
    ones_b = jnp.where(lane < n_bias, 1.0, 0.0).astype(BF16)

    zeros_v = jnp.zeros((half, LANES), BF16)
    ones_v = jnp.ones((half, LANES), BF16)
    for jb in range(s_len // half):
        v2 = v_ref[0, jb * half:(jb + 1) * half, :]
        vbd_ref[jb, 0:half, 0:LANES] = jnp.where(is_a, v2, ones_v)
        vbd_ref[jb, 0:half, LANES:] = zeros_v
        vbd_ref[jb, half:, 0:LANES] = zeros_v
        vbd_ref[jb, half:, LANES:] = jnp.where(is_a, ones_v, v2)

    row_t = lax.broadcasted_iota(jnp.int32, (half, half), 0)
    col_t = lax.broadcasted_iota(jnp.int32, (half, half), 1)
    mask_top = col_t <= row_t
    row_b = lax.broadcasted_iota(jnp.int32, (half, tq), 0)
    col_b = lax.broadcasted_iota(jnp.int32, (half, tq), 1)
    mask_bot = col_b <= row_b + half

    def q_body(qi, carry):
        q0 = pl.multiple_of(qi * tq, tq)
        q2 = q_ref[0, pl.ds(q0, tq), :]
        q_heads = (jnp.where(is_a, q2, ones_a), jnp.where(is_a, ones_b, q2))
        m_ref[...] = jnp.full(m_ref.shape, NEG_BIG, F32)
        acc_ref[...] = jnp.zeros(acc_ref.shape, F32)

        def qk(j, dst):
            k0 = pl.multiple_of(j * tq, tq)
            for h in range(2):
                dst[h] = jnp.dot(q_heads[h], k_aug[h][:, pl.ds(k0, tq)], preferred_element_type=F32)

        def softmax_pv(s_heads, r0, jb0, mask):
            nr, tk = s_heads[0].shape
            nblk = tk // half
            ps, alphas = [], []
            for h in range(2):
                s = s_heads[h]
                if mask is not None:
                    s = jnp.where(mask, s, NEG_BIG)
                m_old = m_ref[h, r0:r0 + nr]
                m_new = jnp.maximum(m_old, jnp.max(s, axis=1, keepdims=True))
                alphas.append(jnp.exp2(m_old - m_new))
                ps.append(jnp.exp2((s - _tile_lanes(m_new, tk // LANES)).astype(BF16)))
                m_ref[h, r0:r0 + nr] = m_new
            p_cat = jnp.concatenate([ps[h][:, i * half:(i + 1) * half]
                                     for i in range(nblk) for h in range(2)], axis=1)
            v_blk = vbd_ref[pl.ds(jb0, nblk)].reshape(nblk * 2 * half, 2 * LANES)
            pv = jnp.dot(p_cat, v_blk, preferred_element_type=F32)
            acc_ref[r0:r0 + nr] = jnp.concatenate(alphas, axis=1) * acc_ref[r0:r0 + nr] + pv

        def step(j, src, dst):
            s_heads = (src[0], src[1])
            qk(j + 1, dst)
            softmax_pv(s_heads, 0, 2 * j, None)

        def diagonal(src):
            softmax_pv((src[0, 0:half, 0:half], src[1, 0:half, 0:half]), 0, 2 * qi, mask_top)
            softmax_pv((src[0, half:, :], src[1, half:, :]), half, 2 * qi, mask_bot)

        qk(0, se_ref)

        def pair(t, c):
            step(2 * t, se_ref, so_ref)
            step(2 * t + 1, so_ref, se_ref)
            return c

        lax.fori_loop(0, qi // 2, pair, 0)

        @pl.when(qi % 2 == 0)
        def _():
            diagonal(se_ref)

        @pl.when(qi % 2 == 1)
        def _():
            step(qi - 1, se_ref, so_ref)
            diagonal(so_ref)

        acc_a = acc_ref[:, :LANES]
        acc_b = acc_ref[:, LANES:]
        out = jnp.where(is_a, acc_a / pltpu.roll(acc_a, hd, 1), acc_b / pltpu.roll(acc_b, hd, 1))
        o_ref[0, pl.ds(q0, tq), :] = out.astype(BF16)
        return carry

    lax.fori_loop(0, nq, q_body, 0)


def _attention(q, kt, v, c_pairs, *, hd, tq):
    b, s, d_attn = q.shape
    n_pairs = d_attn // LANES
    blk = lambda bi, p: (bi, 0, p)
    kern = functools.partial(_attn_kernel, tq=tq, hd=hd)
    return pl.pallas_call(
        kern,
        grid=(b, n_pairs),
        in_specs=[
            pl.BlockSpec((1, s, LANES), blk),
            pl.BlockSpec((1, LANES, s), lambda bi, p: (bi, p, 0)),
            pl.BlockSpec((1, s, LANES), blk),
            pl.BlockSpec((1, 1, 2, s), lambda bi, p: (bi, p, 0, 0)),
        ],
        out_specs=pl.BlockSpec((1, s, LANES), blk),
        out_shape=jax.ShapeDtypeStruct((b, s, d_attn), BF16),
        scratch_shapes=[
            pltpu.VMEM((LANES, s), BF16),
            pltpu.VMEM((LANES, s), BF16),
            pltpu.VMEM((2 * s // tq, tq, 2 * LANES), BF16),
            pltpu.VMEM((tq, 2 * LANES), F32),
            pltpu.VMEM((2, tq, LANES), F32),
            pltpu.VMEM((2, tq, tq), F32),
            pltpu.VMEM((2, tq, tq), F32),
        ],
        compiler_params=pltpu.CompilerParams(
            dimension_semantics=("parallel", "parallel"), vmem_limit_bytes=VMEM_LIMIT),
        name="fox_attn",
    )(q, kt, v, c_pairs)


def _ffn_kernel(*refs, has_proj, final_norm, chunks):
    refs = list(refs)
    x_ref = refs.pop(0)
    if has_proj:
        ya_ref, yr_ref, wo_ref = refs.pop(0), refs.pop(0), refs.pop(0)
    g_ref, win_ref, wout_ref = refs.pop(0), refs.pop(0), refs.pop(0)
    if final_norm:
        gf_ref = refs.pop(0)
    o_ref = refs.pop(0)

    x = x_ref[...]
    if has_proj:
        y = jnp.concatenate([ya_ref[...], yr_ref[...]], axis=1)
        x = x + jnp.dot(y, wo_ref[...], preferred_element_type=F32)
    hn = _rmsnorm(x, g_ref[...]).astype(BF16)
    d_ff = wout_ref.shape[0]
    acc = None
    for lo, hi in zip(chunks[:-1], chunks[1:]):
        gate = jnp.dot(hn, win_ref[:, lo:hi], preferred_element_type=F32)
        up = jnp.dot(hn, win_ref[:, d_ff + lo:d_ff + hi], preferred_element_type=F32)
        act = (gate * _sigmoid(gate) * up).astype(BF16)
        part = jnp.dot(act, wout_ref[lo:hi, :], preferred_element_type=F32)
        acc = part if acc is None else acc + part
    x = x + 0.5 * acc
    if final_norm:
        x = _rmsnorm(x, gf_ref[...])
    o_ref[...] = x


def _ffn(x, g, w_in, w_out, *, proj=None, final_g=None, tm, chunks):
    t, d = x.shape
    const = lambda i: (0, 0)
    rows = lambda i: (i, 0)
    resident = functools.partial(pl.BlockSpec, index_map=const, pipeline_mode=pl.Buffered(1))
    args = [x]
    in_specs = [pl.BlockSpec((tm, d), rows)]
    if proj is not None:
        ya, yr, wo = proj
        args += [ya, yr, wo]
        in_specs += [pl.BlockSpec((tm, ya.shape[1]), rows), pl.BlockSpec((tm, yr.shape[1]), rows),
                     resident(wo.shape)]
    args += [g, w_in, w_out]
    in_specs += [pl.BlockSpec((1, d), const), resident(w_in.shape), resident(w_out.shape)]
    if final_g is not None:
        args.append(final_g)
        in_specs.append(pl.BlockSpec((1, d), const))
    kern = functools.partial(_ffn_kernel, has_proj=proj is not None,
                             final_norm=final_g is not None, chunks=chunks)
    return pl.pallas_call(
        kern,
        grid=(t // tm,),
        in_specs=in_specs,
        out_specs=pl.BlockSpec((tm, d), rows),
        out_shape=jax.ShapeDtypeStruct((t, d), F32),
        compiler_params=pltpu.CompilerParams(
            dimension_semantics=("parallel",), vmem_limit_bytes=VMEM_LIMIT),
        name="ffn",
    )(*args)


def _pick_tile(n, target):
    t = min(n, target)
    while n % t:
        t //= 2
    return t


def _ffn_chunks(d_ff):
    if d_ff % MXU_TILE or d_ff < 2 * MXU_TILE:
        return (0, d_ff)
    tiles = d_ff // MXU_TILE
    return (0, (tiles + 1) // 2 * MXU_TILE, d_ff)


def kernel(x, norm_g, w_in, b_f, conv_w, conv_b, w_rg_a, b_rg_a, w_rg_x, b_rg_x, rg_lambda,
           w_out, w_ffn_in, w_ffn_out, final_g):
    b, s, d = x.shape
    depth = norm_g.shape[0]
    n_heads = b_f.shape[-1]
    d_rec = conv_w.shape[-1]
    n_blocks = w_rg_a.shape[1]
    d_attn = (w_in.shape[-1] - n_heads - 2 * d_rec) // 3
    hd = d_attn // n_heads
    d_ff = w_ffn_out.shape[2]
    assert 2 * hd == LANES and n_heads % 2 == 0 and n_heads <= SUBLANES
    assert d_attn % LANES == 0 and d_rec % LANES == 0 and d % LANES == 0

    tm_mix = _pick_tile(s, 512)
    tq = _pick_tile(s, 512)
    tm_ffn = _pick_tile(b * s, 1024)
    chunks = _ffn_chunks(d_ff)

    o_k, o_v, o_f = d_attn, 2 * d_attn, 3 * d_attn
    o_xr = o_f + n_heads
    tri = jnp.concatenate([jnp.triu(jnp.ones((tm_mix, tm_mix), BF16)),
                           jnp.ones((tm_mix, LANES), BF16)], axis=1)
    eye = jnp.eye(n_blocks, dtype=F32)[:, None, :, None]

    def block_diag(w):
        return (w[:, :, None, :] * eye).reshape(d_rec, d_rec).astype(BF16)

    def mixer_weights(l):
        wl = w_in[l]
        w_rows = jnp.concatenate([wl[:, :o_k], wl[:, o_v:o_f], wl[:, o_xr:]], axis=-1).astype(BF16)
        w_f = jnp.pad(wl[:, o_f:o_xr], ((0, 0), (0, BF16_ROWS - n_heads)))
        w_cols = jnp.concatenate([wl[:, o_k:o_v], w_f], axis=-1).T.astype(BF16)
        bf_rep = jnp.broadcast_to(jnp.pad(b_f[l], (0, SUBLANES - n_heads))[:, None],
                                  (SUBLANES, LANES))
        return (w_rows, w_cols, bf_rep, tri, conv_w[l], conv_b[l][None, :],
                block_diag(w_rg_a[l]), b_rg_a[l][None, :], block_diag(w_rg_x[l]),
                b_rg_x[l][None, :], rg_lambda[l][None, :])

    xf = x.reshape(b * s, d)
    for l in range(depth):
        g = norm_g[l][:, None, :]
        xf = _ffn(xf, g[0], w_ffn_in[l, 0].astype(BF16), w_ffn_out[l, 0].astype(BF16),
                  tm=tm_ffn, chunks=chunks)
        q, kt, v, ct, y_rec = _mixer_in(
            xf.reshape(b, s, d), g[1], *mixer_weights(l),
            d_attn=d_attn, d_rec=d_rec, scale=LOG2E / math.sqrt(hd), tm=tm_mix)
        c_pairs = ct[:, :n_heads, :].reshape(b, n_heads // 2, 2, s)
        y_attn = _attention(q, kt, v, c_pairs, hd=hd, tq=tq)
        last = l == depth - 1
        xf = _ffn(xf, g[2], w_ffn_in[l, 1].astype(BF16), w_ffn_out[l, 1].astype(BF16),
                  proj=(y_attn.reshape(b * s, d_attn), y_rec.reshape(b * s, d_rec),
                        w_out[l].astype(BF16)),
                  final_g=final_g[None, :] if last else None, tm=tm_ffn, chunks=chunks)
    return xf.reshape(b, s, d)
```

```python
import functools
import math

import jax
import jax.numpy as jnp
from jax import lax
from jax.experimental import pallas as pl
from jax.experimental.pallas import tpu as pltpu

EPS = 1e-6
RG_C = 8.0
LANES = 128
SUBLANES = 8
BF16_ROWS = 16
MXU_TILE = 256
NEG_BIG = -1e30
LOG2E = math.log2(math.e)
VMEM_LIMIT = 56 * 1024 * 1024

F32 = jnp.float32
BF16 = jnp.bfloat16
NT_DIMS = (((1,), (1,)), ((), ()))


def _rmsnorm(x, g):
    ms = jnp.mean(x * x, axis=-1, keepdims=True)
    return x * lax.rsqrt(ms + EPS) * g


def _sigmoid(x):
    return 1.0 / (1.0 + jnp.exp(-x))


def _softplus(x):
    return jnp.maximum(x, 0.0) + jnp.log1p(jnp.exp(-jnp.abs(x)))


def _gelu_tanh(x):
    c = math.sqrt(2.0 / math.pi)
    return (0.5 * x) * (1.0 + jnp.tanh(x * (c + (c * 0.044715) * (x * x))))


def _shift_rows(x, k, fill):
    rows = lax.broadcasted_iota(jnp.int32, x.shape, 0)
    return jnp.where(rows >= k, pltpu.roll(x, k, 0), fill)


def _tile_lanes(x, n):
    return jnp.concatenate([x] * n, axis=1)


def _split3(x):
    hi = x.astype(BF16).astype(F32)
    r = x - hi
    mid = r.astype(BF16).astype(F32)
    lo = (r - mid).astype(BF16).astype(F32)
    return hi, mid, lo


def _mixer_in_kernel(x_ref, g_ref, w_ref, wt_ref, bf_ref, tri_ref, cw_ref, cb_ref, wa_ref, ba_ref,
                     wx_ref, bx_ref, lam_ref,
                     q_ref, kt_ref, v_ref, ct_ref, yr_ref,
                     c_carry, h_carry, x_carry, a_s, u_s, hs_s, *, d_attn, d_rec, scale):
    j = pl.program_id(1)
    tm = x_ref.shape[1]
    nb = tm // SUBLANES

    @pl.when(j == 0)
    def _():
        c_carry[...] = jnp.zeros_like(c_carry)
        h_carry[...] = jnp.zeros_like(h_carry)
        x_carry[...] = jnp.zeros_like(x_carry)

    hn = _rmsnorm(x_ref[0], g_ref[...]).astype(BF16)
    z = jnp.dot(hn, w_ref[...], preferred_element_type=F32)
    zt = lax.dot_general(wt_ref[...], hn, NT_DIMS, preferred_element_type=F32)

    q_ref[0] = (z[:, :d_attn] * scale).astype(BF16)
    v_ref[0] = z[:, d_attn:2 * d_attn].astype(BF16)
    kt_ref[0] = zt[:d_attn].astype(BF16)
    o_xr = 2 * d_attn
    o_gr = o_xr + d_rec
    xr = z[:, o_xr:o_gr]
    gr = z[:, o_gr:]

    f_logit = zt[d_attn:d_attn + SUBLANES] + _tile_lanes(bf_ref[...], tm // LANES)
    log_f = -_softplus(-f_logit)
    parts = jnp.concatenate(_split3(log_f) + (jnp.zeros_like(log_f),), axis=0).astype(BF16)
    cs = jnp.dot(parts, tri_ref[...], preferred_element_type=F32)
    cs = cs[0:SUBLANES] + cs[SUBLANES:2 * SUBLANES] + cs[2 * SUBLANES:3 * SUBLANES]
    carry = c_carry[...]
    ct_ref[0] = cs[:, :tm] + _tile_lanes(carry, tm // LANES)
    c_carry[...] = carry + cs[:, tm:]

    width = cw_ref.shape[0]
    ext = jnp.concatenate([x_carry[...], xr], axis=0)
    x_carry[...] = xr[tm - SUBLANES:, :]
    xc = xr * cw_ref[width - 1:width, :] + cb_ref[...]
    for s in range(1, width):
        xc = xc + pltpu.roll(ext, s, 0)[SUBLANES:, :] * cw_ref[width - 1 - s:width - s, :]

    xcb = xc.astype(BF16)
    r = _sigmoid(jnp.dot(xcb, wa_ref[...], preferred_element_type=F32) + ba_ref[...])
    i = _sigmoid(jnp.dot(xcb, wx_ref[...], preferred_element_type=F32) + bx_ref[...])
    log_a = (-RG_C * _softplus(-lam_ref[...])) * r
    a = jnp.exp(log_a)
    u = jnp.sqrt(-jnp.tanh(log_a) * (1.0 + a * a)) * (i * xc)

    a3 = a.reshape(nb, SUBLANES, d_rec)
    u3 = u.reshape(nb, SUBLANES, d_rec)
    sub = lax.broadcasted_iota(jnp.int32, a3.shape, 1)
    k = 1
    while k < SUBLANES:
        keep = sub >= k
        u3 = u3 + a3 * jnp.where(keep, pltpu.roll(u3, k, 1), 0.0)
        a3 = a3 * jnp.where(keep, pltpu.roll(a3, k, 1), 1.0)
        k *= 2
    a2 = a3.reshape(tm, d_rec)
    u2 = u3.reshape(tm, d_rec)
    n_lt = d_rec // LANES
    for c in range(n_lt):
        a_s[c] = a2[:, c * LANES:(c + 1) * LANES]
        u_s[c] = u2[:, c * LANES:(c + 1) * LANES]
    last_rows = pl.ds(SUBLANES - 1, nb, stride=SUBLANES)
    ab = jnp.concatenate([a_s[c, last_rows, :] for c in range(n_lt)], axis=1)
    ub = jnp.concatenate([u_s[c, last_rows, :] for c in range(n_lt)], axis=1)
    k = 1
    while k < nb:
        ub = ub + ab * _shift_rows(ub, k, 0.0)
        ab = ab * _shift_rows(ab, k, 1.0)
        k *= 2
    h_in = h_carry[...]
    h_end = ab * h_in + ub
    h_carry[...] = h_end[nb - 1:nb, :]
    hs_s[...] = _shift_rows(h_end, 1, h_in)
    for b in range(nb):
        rows = slice(b * SUBLANES, (b + 1) * SUBLANES)
        for c in range(n_lt):
            u_s[c, rows, :] = (a_s[c, rows, :] * hs_s[b:b + 1, c * LANES:(c + 1) * LANES]
                               + u_s[c, rows, :])
    h = jnp.concatenate([u_s[c] for c in range(n_lt)], axis=1)

    yr_ref[0] = (h * _gelu_tanh(gr)).astype(BF16)


def _mixer_in(x, g, w, wt, bf_rep, tri, conv_w, conv_b, wa, ba, wx, bx, lam, *, d_attn, d_rec, scale, tm):
    b, s, d = x.shape
    const = lambda *_: (0, 0)
    row3 = lambda bi, j: (bi, j, 0)
    col3 = lambda bi, j: (bi, 0, j)
    kern = functools.partial(_mixer_in_kernel, d_attn=d_attn, d_rec=d_rec, scale=scale)
    return pl.pallas_call(
        kern,
        grid=(b, s // tm),
        in_specs=[
            pl.BlockSpec((1, tm, d), row3),
            pl.BlockSpec((1, d), const),
            pl.BlockSpec(w.shape, const),
            pl.BlockSpec(wt.shape, const),
            pl.BlockSpec((SUBLANES, LANES), const),
            pl.BlockSpec(tri.shape, const),
            pl.BlockSpec(conv_w.shape, const),
            pl.BlockSpec((1, d_rec), const),
            pl.BlockSpec((d_rec, d_rec), const),
            pl.BlockSpec((1, d_rec), const),
            pl.BlockSpec((d_rec, d_rec), const),
            pl.BlockSpec((1, d_rec), const),
            pl.BlockSpec((1, d_rec), const),
        ],
        out_specs=[
            pl.BlockSpec((1, tm, d_attn), row3),
            pl.BlockSpec((1, d_attn, tm), col3),
            pl.BlockSpec((1, tm, d_attn), row3),
            pl.BlockSpec((1, SUBLANES, tm), col3),
            pl.BlockSpec((1, tm, d_rec), row3),
        ],
        out_shape=[
            jax.ShapeDtypeStruct((b, s, d_attn), BF16),
            jax.ShapeDtypeStruct((b, d_attn, s), BF16),
            jax.ShapeDtypeStruct((b, s, d_attn), BF16),
            jax.ShapeDtypeStruct((b, SUBLANES, s), F32),
            jax.ShapeDtypeStruct((b, s, d_rec), BF16),
        ],
        scratch_shapes=[
            pltpu.VMEM((SUBLANES, LANES), F32),
            pltpu.VMEM((1, d_rec), F32),
            pltpu.VMEM((SUBLANES, d_rec), F32),
            pltpu.VMEM((d_rec // LANES, tm, LANES), F32),
            pltpu.VMEM((d_rec // LANES, tm, LANES), F32),
            pltpu.VMEM((tm // SUBLANES, d_rec), F32),
        ],
        compiler_params=pltpu.CompilerParams(
            dimension_semantics=("parallel", "arbitrary"), vmem_limit_bytes=VMEM_LIMIT),
        name="mixer_in",
    )(x, g, w, wt, bf_rep, tri, conv_w, conv_b, wa, ba, wx, bx, lam)


def _attn_kernel(q_ref, kt_ref, v_ref, c_ref, o_ref, ka_ref, kb_ref, vbd_ref, acc_ref, m_ref,
                 se_ref, so_ref, *, tq, hd):
    s_len = q_ref.shape[1]
    nq = s_len // tq
    half = tq // 2
    lane = lax.broadcasted_iota(jnp.int32, (1, LANES), 1)
    is_a = lane < hd

    sub = lax.broadcasted_iota(jnp.int32, (BF16_ROWS, s_len), 0)

    def bias_rows(h):
        hi, mid, lo = _split3(c_ref[0, 0, h:h + 1, :] * (-LOG2E))
        rows = jnp.where(sub == 0, hi, jnp.where(sub == 1, mid, jnp.where(sub == 2, lo, 0.0)))
        return rows.astype(BF16)

    zeros_k = jnp.zeros((hd - BF16_ROWS, s_len), BF16)
    ka_ref[0:hd, :] = kt_ref[0, 0:hd, :]
    ka_ref[hd:hd + BF16_ROWS, :] = bias_rows(0)
    ka_ref[hd + BF16_ROWS:, :] = zeros_k
    kb_ref[0:BF16_ROWS, :] = bias_rows(1)
    kb_ref[BF16_ROWS:hd, :] = zeros_k
    kb_ref[hd:, :] = kt_ref[0, hd:, :]
    k_aug = (ka_ref, kb_ref)
    n_bias = 3
    ones_a = jnp.where((lane >= hd) & (lane < hd + n_bias), 1.0, 0.0).astype(BF16)
    ones_b = jnp.where(lane < n_bias, 1.0, 0.0).astype(BF16)

    zeros_v = jnp.zeros((half, LANES), BF16)
    ones_v = jnp.ones((half, LANES), BF16)
    for jb in range(s_len // half):
        v2 = v_ref[0, jb * half:(jb + 1) * half, :]
        vbd_ref[jb, 0:half, 0:LANES] = jnp.where(is_a, v2, ones_v)
        vbd_ref[jb, 0:half, LANES:] = zeros_v
        vbd_ref[jb, half:, 0:LANES] = zeros_v
        vbd_ref[jb, half:, LANES:] = jnp.where(is_a, ones_v, v2)

    row_t = lax.broadcasted_iota(jnp.int32, (half, half), 0)
    col_t = lax.broadcasted_iota(jnp.int32, (half, half), 1)
    mask_top = col_t <= row_t
    row_b = lax.broadcasted_iota(jnp.int32, (half, tq), 0)
    col_b = lax.broadcasted_iota(jnp.int32, (half, tq), 1)
    mask_bot = col_b <= row_b + half

    def q_body(qi, carry):
        q0 = pl.multiple_of(qi * tq, tq)
        q2 = q_ref[0, pl.ds(q0, tq), :]
        q_heads = (jnp.where(is_a, q2, ones_a), jnp.where(is_a, ones_b, q2))
        m_ref[...] = jnp.full(m_ref.shape, NEG_BIG, F32)
        acc_ref[...] = jnp.zeros(acc_ref.shape, F32)

        def qk(j, dst):
            k0 = pl.multiple_of(j * tq, tq)
            for h in range(2):
                dst[h] = jnp.dot(q_heads[h], k_aug[h][:, pl.ds(k0, tq)], preferred_element_type=F32)

        def softmax_pv(s_heads, r0, jb0, mask):
            nr, tk = s_heads[0].shape
            nblk = tk // half
            ps, alphas = [], []
            for h in range(2):
                s = s_heads[h]
                if mask is not None:
                    s = jnp.where(mask, s, NEG_BIG)
                m_old = m_ref[h, r0:r0 + nr]
                m_new = jnp.maximum(m_old, jnp.max(s, axis=1, keepdims=True))
                alphas.append(jnp.exp2(m_old - m_new))
                ps.append(jnp.exp2((s - _tile_lanes(m_new, tk // LANES)).astype(BF16)))
                m_ref[h, r0:r0 + nr] = m_new
            p_cat = jnp.concatenate([ps[h][:, i * half:(i + 1) * half]
                                     for i in range(nblk) for h in range(2)], axis=1)
            v_blk = vbd_ref[pl.ds(jb0, nblk)].reshape(nblk * 2 * half, 2 * LANES)
            pv = jnp.dot(p_cat, v_blk, preferred_element_type=F32)
            acc_ref[r0:r0 + nr] = jnp.concatenate(alphas, axis=1) * acc_ref[r0:r0 + nr] + pv

        def step(j, src, dst):
            s_heads = (src[0], src[1])
            qk(j + 1, dst)
            softmax_pv(s_heads, 0, 2 * j, None)

        def diagonal(src):
            softmax_pv((src[0, 0:half, 0:half], src[1, 0:half, 0:half]), 0, 2 * qi, mask_top)
            softmax_pv((src[0, half:, :], src[1, half:, :]), half, 2 * qi, mask_bot)

        qk(0, se_ref)

        def pair(t, c):
            step(2 * t, se_ref, so_ref)
            step(2 * t + 1, so_ref, se_ref)
            return c

        lax.fori_loop(0, qi // 2, pair, 0)

        @pl.when(qi % 2 == 0)
        def _():
            diagonal(se_ref)

        @pl.when(qi % 2 == 1)
        def _():
            step(qi - 1, se_ref, so_ref)
            diagonal(so_ref)

        acc_a = acc_ref[:, :LANES]
        acc_b = acc_ref[:, LANES:]
        num = jnp.where(is_a, acc_a, acc_b)
        den = pltpu.roll(jnp.where(is_a, acc_b, acc_a), hd, 1)
        o_ref[0, pl.ds(q0, tq), :] = (num / den).astype(BF16)
        return carry

    lax.fori_loop(0, nq, q_body, 0)


def _attention(q, kt, v, c_pairs, *, hd, tq):
    b, s, d_attn = q.shape
    n_pairs = d_attn // LANES
    blk = lambda bi, p: (bi, 0, p)
    kern = functools.partial(_attn_kernel, tq=tq, hd=hd)
    return pl.pallas_call(
        kern,
        grid=(b, n_pairs),
        in_specs=[
            pl.BlockSpec((1, s, LANES), blk),
            pl.BlockSpec((1, LANES, s), lambda bi, p: (bi, p, 0)),
            pl.BlockSpec((1, s, LANES), blk),
            pl.BlockSpec((1, 1, 2, s), lambda bi, p: (bi, p, 0, 0)),
        ],
        out_specs=pl.BlockSpec((1, s, LANES), blk),
        out_shape=jax.ShapeDtypeStruct((b, s, d_attn), BF16),
        scratch_shapes=[
            pltpu.VMEM((LANES, s), BF16),
            pltpu.VMEM((LANES, s), BF16),
            pltpu.VMEM((2 * s // tq, tq, 2 * LANES), BF16),
            pltpu.VMEM((tq, 2 * LANES), F32),
            pltpu.VMEM((2, tq, LANES), F32),
            pltpu.VMEM((2, tq, tq), F32),
            pltpu.VMEM((2, tq, tq), F32),
        ],
        compiler_params=pltpu.CompilerParams(
            dimension_semantics=("parallel", "parallel"), vmem_limit_bytes=VMEM_LIMIT),
        name="fox_attn",
    )(q, kt, v, c_pairs)


def _ffn_kernel(*refs, has_proj, final_norm, chunks):
    refs = list(refs)
    x_ref = refs.pop(0)
    if has_proj:
        ya_ref, yr_ref, wo_ref = refs.pop(0), refs.pop(0), refs.pop(0)
    g_ref, win_ref, wout_ref = refs.pop(0), refs.pop(0), refs.pop(0)
    if final_norm:
        gf_ref = refs.pop(0)
    o_ref = refs.pop(0)

    x = x_ref[...]
    if has_proj:
        y = jnp.concatenate([ya_ref[...], yr_ref[...]], axis=1)
        x = x + jnp.dot(y, wo_ref[...], preferred_element_type=F32)
    hn = _rmsnorm(x, g_ref[...]).astype(BF16)
    d_ff = wout_ref.shape[0]
    acc = None
    for lo, hi in zip(chunks[:-1], chunks[1:]):
        gate = jnp.dot(hn, win_ref[:, lo:hi], preferred_element_type=F32)
        up = jnp.dot(hn, win_ref[:, d_ff + lo:d_ff + hi], preferred_element_type=F32)
        act = (gate * _sigmoid(gate) * up).astype(BF16)
        part = jnp.dot(act, wout_ref[lo:hi, :], preferred_element_type=F32)
        acc = part if acc is None else acc + part
    x = x + 0.5 * acc
    if final_norm:
        x = _rmsnorm(x, gf_ref[...])
    o_ref[...] = x


def _ffn(x, g, w_in, w_out, w_idx, *, proj=None, final_g=None, tm, chunks):
    t, d = x.shape
    const = lambda i: (0, 0)
    rows = lambda i: (i, 0)

    def resident(w, lead):
        block = (None,) * len(lead) + w.shape[len(lead):]
        return pl.BlockSpec(block, lambda i: lead + (0, 0), pipeline_mode=pl.Buffered(1))

    args = [x]
    in_specs = [pl.BlockSpec((tm, d), rows)]
    if proj is not None:
        ya, yr, wo = proj
        args += [ya, yr, wo]
        in_specs += [pl.BlockSpec((tm, ya.shape[1]), rows), pl.BlockSpec((tm, yr.shape[1]), rows),
                     resident(wo, w_idx[:1])]
    args += [g, w_in, w_out]
    in_specs += [pl.BlockSpec((1, d), const), resident(w_in, w_idx), resident(w_out, w_idx)]
    if final_g is not None:
        args.append(final_g)
        in_specs.append(pl.BlockSpec((1, d), const))
    kern = functools.partial(_ffn_kernel, has_proj=proj is not None,
                             final_norm=final_g is not None, chunks=chunks)
    return pl.pallas_call(
        kern,
        grid=(t // tm,),
        in_specs=in_specs,
        out_specs=pl.BlockSpec((tm, d), rows),
        out_shape=jax.ShapeDtypeStruct((t, d), F32),
        compiler_params=pltpu.CompilerParams(
            dimension_semantics=("parallel",), vmem_limit_bytes=VMEM_LIMIT),
        name="ffn",
    )(*args)


def _pick_tile(n, target):
    t = min(n, target)
    while n % t:
        t //= 2
    return t


def _ffn_chunks(d_ff):
    if d_ff % MXU_TILE or d_ff < 2 * MXU_TILE:
        return (0, d_ff)
    tiles = d_ff // MXU_TILE
    return (0, (tiles + 1) // 2 * MXU_TILE, d_ff)


def kernel(x, norm_g, w_in, b_f, conv_w, conv_b, w_rg_a, b_rg_a, w_rg_x, b_rg_x, rg_lambda,
           w_out, w_ffn_in, w_ffn_out, final_g):
    b, s, d = x.shape
    depth = norm_g.shape[0]
    n_heads = b_f.shape[-1]
    d_rec = conv_w.shape[-1]
    n_blocks = w_rg_a.shape[1]
    d_attn = (w_in.shape[-1] - n_heads - 2 * d_rec) // 3
    hd = d_attn // n_heads
    d_ff = w_ffn_out.shape[2]
    assert 2 * hd == LANES and n_heads % 2 == 0 and n_heads <= SUBLANES
    assert d_attn % LANES == 0 and d_rec % LANES == 0 and d % LANES == 0

    tm_mix = _pick_tile(s, 512)
    tq = _pick_tile(s, 512)
    tm_ffn = _pick_tile(b * s, 1024)
    chunks = _ffn_chunks(d_ff)

    o_k, o_v, o_f = d_attn, 2 * d_attn, 3 * d_attn
    o_xr = o_f + n_heads
    tri = jnp.concatenate([jnp.triu(jnp.ones((tm_mix, tm_mix), BF16)),
                           jnp.ones((tm_mix, LANES), BF16)], axis=1)
    eye = jnp.eye(n_blocks, dtype=F32)[:, None, :, None]

    def block_diag(w):
        return (w[:, :, None, :] * eye).reshape(d_rec, d_rec).astype(BF16)

    def mixer_weights(l):
        wl = w_in[l]
        w_rows = jnp.concatenate([wl[:, :o_k], wl[:, o_v:o_f], wl[:, o_xr:]], axis=-1).astype(BF16)
        w_f = jnp.pad(wl[:, o_f:o_xr], ((0, 0), (0, BF16_ROWS - n_heads)))
        w_cols = jnp.concatenate([wl[:, o_k:o_v], w_f], axis=-1).T.astype(BF16)
        bf_rep = jnp.broadcast_to(jnp.pad(b_f[l], (0, SUBLANES - n_heads))[:, None],
                                  (SUBLANES, LANES))
        return (w_rows, w_cols, bf_rep, tri, conv_w[l], conv_b[l][None, :],
                block_diag(w_rg_a[l]), b_rg_a[l][None, :], block_diag(w_rg_x[l]),
                b_rg_x[l][None, :], rg_lambda[l][None, :])

    w_out_b = w_out.astype(BF16)
    w_ffn_in_b = w_ffn_in.astype(BF16)
    w_ffn_out_b = w_ffn_out.astype(BF16)

    xf = x.reshape(b * s, d)
    for l in range(depth):
        g = norm_g[l][:, None, :]
        xf = _ffn(xf, g[0], w_ffn_in_b, w_ffn_out_b, (l, 0), tm=tm_ffn, chunks=chunks)
        q, kt, v, ct, y_rec = _mixer_in(
            xf.reshape(b, s, d), g[1], *mixer_weights(l),
            d_attn=d_attn, d_rec=d_rec, scale=LOG2E / math.sqrt(hd), tm=tm_mix)
        c_pairs = ct[:, :n_heads, :].reshape(b, n_heads // 2, 2, s)
        y_attn = _attention(q, kt, v, c_pairs, hd=hd, tq=tq)
        last = l == depth - 1
        xf = _ffn(xf, g[2], w_ffn_in_b, w_ffn_out_b, (l, 1),
                  proj=(y_attn.reshape(b * s, d_attn), y_rec.reshape(b * s, d_rec), w_out_b),
                  final_g=final_g[None, :] if last else None, tm=tm_ffn, chunks=chunks)
    return xf.reshape(b, s, d)
```

```python
import functools
import math

import jax
import jax.numpy as jnp
from jax import lax
from jax.experimental import pallas as pl
from jax.experimental.pallas import tpu as pltpu

EPS = 1e-6
RG_C = 8.0
LANES = 128
SUBLANES = 8
BF16_ROWS = 16
MXU_TILE = 256
UNROLL = 8
NEG_BIG = -1e30
LOG2E = math.log2(math.e)
VMEM_LIMIT = 56 * 1024 * 1024

F32 = jnp.float32
BF16 = jnp.bfloat16
NT_DIMS = (((1,), (1,)), ((), ()))


def _rmsnorm(x, g):
    ms = jnp.mean(x * x, axis=-1, keepdims=True)
    return x * lax.rsqrt(ms + EPS) * g


def _sigmoid(x):
    return 1.0 / (1.0 + jnp.exp(-x))


def _softplus(x):
    return jnp.maximum(x, 0.0) + jnp.log1p(jnp.exp(-jnp.abs(x)))


def _gelu_tanh(x):
    c = math.sqrt(2.0 / math.pi)
    return (0.5 * x) * (1.0 + jnp.tanh(x * (c + (c * 0.044715) * (x * x))))


def _shift_rows(x, k, fill):
    rows = lax.broadcasted_iota(jnp.int32, x.shape, 0)
    return jnp.where(rows >= k, pltpu.roll(x, k, 0), fill)


def _tile_lanes(x, n):
    return jnp.concatenate([x] * n, axis=1)


def _split3(x):
    hi = x.astype(BF16).astype(F32)
    r = x - hi
    mid = r.astype(BF16).astype(F32)
    lo = (r - mid).astype(BF16).astype(F32)
    return hi, mid, lo


def _mixer_in_kernel(x_ref, g_ref, w_ref, wt_ref, bf_ref, tri_ref, cw_ref, cb_ref, wa_ref, ba_ref,
                     wx_ref, bx_ref, lam_ref,
                     q_ref, kt_ref, v_ref, ct_ref, yr_ref,
                     c_carry, h_carry, x_carry, a_s, u_s, hs_s, *, d_attn, d_rec, scale):
    j = pl.program_id(1)
    tm = x_ref.shape[1]
    nb = tm // SUBLANES

    @pl.when(j == 0)
    def _():
        c_carry[...] = jnp.zeros_like(c_carry)
        h_carry[...] = jnp.zeros_like(h_carry)
        x_carry[...] = jnp.zeros_like(x_carry)

    hn = _rmsnorm(x_ref[0], g_ref[...]).astype(BF16)
    z = jnp.dot(hn, w_ref[...], preferred_element_type=F32)
    zt = lax.dot_general(wt_ref[...], hn, NT_DIMS, preferred_element_type=F32)

    q_ref[0] = (z[:, :d_attn] * scale).astype(BF16)
    v_ref[0] = z[:, d_attn:2 * d_attn].astype(BF16)
    kt_ref[0] = zt[:d_attn].astype(BF16)
    o_xr = 2 * d_attn
    o_gr = o_xr + d_rec
    xr = z[:, o_xr:o_gr]
    gr = z[:, o_gr:]

    f_logit = zt[d_attn:d_attn + SUBLANES] + _tile_lanes(bf_ref[...], tm // LANES)
    log_f = -_softplus(-f_logit)
    parts = jnp.concatenate(_split3(log_f) + (jnp.zeros_like(log_f),), axis=0).astype(BF16)
    cs = jnp.dot(parts, tri_ref[...], preferred_element_type=F32)
    cs = cs[0:SUBLANES] + cs[SUBLANES:2 * SUBLANES] + cs[2 * SUBLANES:3 * SUBLANES]
    carry = c_carry[...]
    ct_ref[0] = cs[:, :tm] + _tile_lanes(carry, tm // LANES)
    c_carry[...] = carry + cs[:, tm:]

    width = cw_ref.shape[0]
    ext = jnp.concatenate([x_carry[...], xr], axis=0)
    x_carry[...] = xr[tm - SUBLANES:, :]
    xc = xr * cw_ref[width - 1:width, :] + cb_ref[...]
    for s in range(1, width):
        xc = xc + pltpu.roll(ext, s, 0)[SUBLANES:, :] * cw_ref[width - 1 - s:width - s, :]

    xcb = xc.astype(BF16)
    r = _sigmoid(jnp.dot(xcb, wa_ref[...], preferred_element_type=F32) + ba_ref[...])
    i = _sigmoid(jnp.dot(xcb, wx_ref[...], preferred_element_type=F32) + bx_ref[...])
    log_a = (-RG_C * _softplus(-lam_ref[...])) * r
    a = jnp.exp(log_a)
    u = jnp.sqrt(-jnp.tanh(log_a) * (1.0 + a * a)) * (i * xc)

    a3 = a.reshape(nb, SUBLANES, d_rec)
    u3 = u.reshape(nb, SUBLANES, d_rec)
    sub = lax.broadcasted_iota(jnp.int32, a3.shape, 1)
    k = 1
    while k < SUBLANES:
        keep = sub >= k
        u3 = u3 + a3 * jnp.where(keep, pltpu.roll(u3, k, 1), 0.0)
        a3 = a3 * jnp.where(keep, pltpu.roll(a3, k, 1), 1.0)
        k *= 2
    a2 = a3.reshape(tm, d_rec)
    u2 = u3.reshape(tm, d_rec)
    n_lt = d_rec // LANES
    for c in range(n_lt):
        a_s[c] = a2[:, c * LANES:(c + 1) * LANES]
        u_s[c] = u2[:, c * LANES:(c + 1) * LANES]
    last_rows = pl.ds(SUBLANES - 1, nb, stride=SUBLANES)
    ab = jnp.concatenate([a_s[c, last_rows, :] for c in range(n_lt)], axis=1)
    ub = jnp.concatenate([u_s[c, last_rows, :] for c in range(n_lt)], axis=1)
    k = 1
    while k < nb:
        ub = ub + ab * _shift_rows(ub, k, 0.0)
        ab = ab * _shift_rows(ab, k, 1.0)
        k *= 2
    h_in = h_carry[...]
    h_end = ab * h_in + ub
    h_carry[...] = h_end[nb - 1:nb, :]
    hs_s[...] = _shift_rows(h_end, 1, h_in)
    for b in range(nb):
        rows = slice(b * SUBLANES, (b + 1) * SUBLANES)
        for c in range(n_lt):
            u_s[c, rows, :] = (a_s[c, rows, :] * hs_s[b:b + 1, c * LANES:(c + 1) * LANES]
                               + u_s[c, rows, :])
    h = jnp.concatenate([u_s[c] for c in range(n_lt)], axis=1)

    yr_ref[0] = (h * _gelu_tanh(gr)).astype(BF16)


def _mixer_in(x, g, w, wt, bf_rep, tri, conv_w, conv_b, wa, ba, wx, bx, lam, *, d_attn, d_rec, scale, tm):
    b, s, d = x.shape
    const = lambda *_: (0, 0)
    row3 = lambda bi, j: (bi, j, 0)
    col3 = lambda bi, j: (bi, 0, j)
    kern = functools.partial(_mixer_in_kernel, d_attn=d_attn, d_rec=d_rec, scale=scale)
    return pl.pallas_call(
        kern,
        grid=(b, s // tm),
        in_specs=[
            pl.BlockSpec((1, tm, d), row3),
            pl.BlockSpec((1, d), const),
            pl.BlockSpec(w.shape, const),
            pl.BlockSpec(wt.shape, const),
            pl.BlockSpec((SUBLANES, LANES), const),
            pl.BlockSpec(tri.shape, const),
            pl.BlockSpec(conv_w.shape, const),
            pl.BlockSpec((1, d_rec), const),
            pl.BlockSpec((d_rec, d_rec), const),
            pl.BlockSpec((1, d_rec), const),
            pl.BlockSpec((d_rec, d_rec), const),
            pl.BlockSpec((1, d_rec), const),
            pl.BlockSpec((1, d_rec), const),
        ],
        out_specs=[
            pl.BlockSpec((1, tm, d_attn), row3),
            pl.BlockSpec((1, d_attn, tm), col3),
            pl.BlockSpec((1, tm, d_attn), row3),
            pl.BlockSpec((1, SUBLANES, tm), col3),
            pl.BlockSpec((1, tm, d_rec), row3),
        ],
        out_shape=[
            jax.ShapeDtypeStruct((b, s, d_attn), BF16),
            jax.ShapeDtypeStruct((b, d_attn, s), BF16),
            jax.ShapeDtypeStruct((b, s, d_attn), BF16),
            jax.ShapeDtypeStruct((b, SUBLANES, s), F32),
            jax.ShapeDtypeStruct((b, s, d_rec), BF16),
        ],
        scratch_shapes=[
            pltpu.VMEM((SUBLANES, LANES), F32),
            pltpu.VMEM((1, d_rec), F32),
            pltpu.VMEM((SUBLANES, d_rec), F32),
            pltpu.VMEM((d_rec // LANES, tm, LANES), F32),
            pltpu.VMEM((d_rec // LANES, tm, LANES), F32),
            pltpu.VMEM((tm // SUBLANES, d_rec), F32),
        ],
        compiler_params=pltpu.CompilerParams(
            dimension_semantics=("parallel", "arbitrary"), vmem_limit_bytes=VMEM_LIMIT),
        name="mixer_in",
    )(x, g, w, wt, bf_rep, tri, conv_w, conv_b, wa, ba, wx, bx, lam)


def _attn_kernel(q_ref, kt_ref, v_ref, c_ref, o_ref, ka_ref, kb_ref, vbd_ref, acc_ref, m_ref,
                 se_ref, so_ref, *, tq, hd):
    s_len = q_ref.shape[1]
    nq = s_len // tq
    half = tq // 2
    lane = lax.broadcasted_iota(jnp.int32, (1, LANES), 1)
    is_a = lane < hd

    sub = lax.broadcasted_iota(jnp.int32, (BF16_ROWS, s_len), 0)

    def bias_rows(h):
        hi, mid, lo = _split3(c_ref[0, 0, h:h + 1, :] * (-LOG2E))
        rows = jnp.where(sub == 0, hi, jnp.where(sub == 1, mid, jnp.where(sub == 2, lo, 0.0)))
        return rows.astype(BF16)

    zeros_k = jnp.zeros((hd - BF16_ROWS, s_len), BF16)
    ka_ref[0:hd, :] = kt_ref[0, 0:hd, :]
    ka_ref[hd:hd + BF16_ROWS, :] = bias_rows(0)
    ka_ref[hd + BF16_ROWS:, :] = zeros_k
    kb_ref[0:BF16_ROWS, :] = bias_rows(1)
    kb_ref[BF16_ROWS:hd, :] = zeros_k
    kb_ref[hd:, :] = kt_ref[0, hd:, :]
    k_aug = (ka_ref, kb_ref)
    n_bias = 3
    ones_a = jnp.where((lane >= hd) & (lane < hd + n_bias), 1.0, 0.0).astype(BF16)
    ones_b = jnp.where(lane < n_bias, 1.0, 0.0).astype(BF16)

    zeros_v = jnp.zeros((half, LANES), BF16)
    ones_v = jnp.ones((half, LANES), BF16)
    for jb in range(s_len // half):
        v2 = v_ref[0, jb * half:(jb + 1) * half, :]
        vbd_ref[jb, 0:half, 0:LANES] = jnp.where(is_a, v2, ones_v)
        vbd_ref[jb, 0:half, LANES:] = zeros_v
        vbd_ref[jb, half:, 0:LANES] = zeros_v
        vbd_ref[jb, half:, LANES:] = jnp.where(is_a, ones_v, v2)

    row_t = lax.broadcasted_iota(jnp.int32, (half, half), 0)
    col_t = lax.broadcasted_iota(jnp.int32, (half, half), 1)
    mask_top = col_t <= row_t
    row_b = lax.broadcasted_iota(jnp.int32, (half, tq), 0)
    col_b = lax.broadcasted_iota(jnp.int32, (half, tq), 1)
    mask_bot = col_b <= row_b + half

    def q_body(qi, carry):
        q0 = pl.multiple_of(qi * tq, tq)
        q2 = q_ref[0, pl.ds(q0, tq), :]
        q_heads = (jnp.where(is_a, q2, ones_a), jnp.where(is_a, ones_b, q2))
        m_ref[...] = jnp.full(m_ref.shape, NEG_BIG, F32)
        acc_ref[...] = jnp.zeros(acc_ref.shape, F32)

        def qk(j, dst):
            k0 = pl.multiple_of(j * tq, tq)
            for h in range(2):
                dst[h] = jnp.dot(q_heads[h], k_aug[h][:, pl.ds(k0, tq)], preferred_element_type=F32)

        def softmax_pv(s_heads, r0, jb0, mask):
            nr, tk = s_heads[0].shape
            nblk = tk // half
            ps, alphas = [], []
            for h in range(2):
                s = s_heads[h]
                if mask is not None:
                    s = jnp.where(mask, s, NEG_BIG)
                m_old = m_ref[h, r0:r0 + nr]
                m_new = jnp.maximum(m_old, jnp.max(s, axis=1, keepdims=True))
                alphas.append(jnp.exp2(m_old - m_new))
                ps.append(jnp.exp2((s - _tile_lanes(m_new, tk // LANES)).astype(BF16)))
                m_ref[h, r0:r0 + nr] = m_new
            p_cat = jnp.concatenate([ps[h][:, i * half:(i + 1) * half]
                                     for i in range(nblk) for h in range(2)], axis=1)
            v_blk = vbd_ref[pl.ds(jb0, nblk)].reshape(nblk * 2 * half, 2 * LANES)
            pv = jnp.dot(p_cat, v_blk, preferred_element_type=F32)
            acc_ref[r0:r0 + nr] = jnp.concatenate(alphas, axis=1) * acc_ref[r0:r0 + nr] + pv

        def step(j, src, dst):
            s_heads = (src[0], src[1])
            qk(j + 1, dst)
            softmax_pv(s_heads, 0, 2 * j, None)

        def diagonal(src):
            softmax_pv((src[0, 0:half, 0:half], src[1, 0:half, 0:half]), 0, 2 * qi, mask_top)
            softmax_pv((src[0, half:, :], src[1, half:, :]), half, 2 * qi, mask_bot)

        qk(0, se_ref)

        bufs = (se_ref, so_ref)

        def steps(j0, n):
            for i in range(n):
                step(j0 + i, bufs[i % 2], bufs[(i + 1) % 2])

        def unrolled(t, c):
            steps(UNROLL * t, UNROLL)
            return c

        lax.fori_loop(0, qi // UNROLL, unrolled, 0)
        base = (qi // UNROLL) * UNROLL
        for rem in range(UNROLL):
            @pl.when(qi - base == rem)
            def _(rem=rem):
                steps(base, rem)
                diagonal(bufs[rem % 2])

        acc_a = acc_ref[:, :LANES]
        acc_b = acc_ref[:, LANES:]
        num = jnp.where(is_a, acc_a, acc_b)
        den = pltpu.roll(jnp.where(is_a, acc_b, acc_a), hd, 1)
        o_ref[0, pl.ds(q0, tq), :] = (num / den).astype(BF16)
        return carry

    lax.fori_loop(0, nq, q_body, 0)


def _attention(q, kt, v, c_pairs, *, hd, tq):
    b, s, d_attn = q.shape
    n_pairs = d_attn // LANES
    blk = lambda bi, p: (bi, 0, p)
    kern = functools.partial(_attn_kernel, tq=tq, hd=hd)
    return pl.pallas_call(
        kern,
        grid=(b, n_pairs),
        in_specs=[
            pl.BlockSpec((1, s, LANES), blk),
            pl.BlockSpec((1, LANES, s), lambda bi, p: (bi, p, 0)),
            pl.BlockSpec((1, s, LANES), blk),
            pl.BlockSpec((1, 1, 2, s), lambda bi, p: (bi, p, 0, 0)),
        ],
        out_specs=pl.BlockSpec((1, s, LANES), blk),
        out_shape=jax.ShapeDtypeStruct((b, s, d_attn), BF16),
        scratch_shapes=[
            pltpu.VMEM((LANES, s), BF16),
            pltpu.VMEM((LANES, s), BF16),
            pltpu.VMEM((2 * s // tq, tq, 2 * LANES), BF16),
            pltpu.VMEM((tq, 2 * LANES), F32),
            pltpu.VMEM((2, tq, LANES), F32),
            pltpu.VMEM((2, tq, tq), F32),
            pltpu.VMEM((2, tq, tq), F32),
        ],
        compiler_params=pltpu.CompilerParams(
            dimension_semantics=("parallel", "parallel"), vmem_limit_bytes=VMEM_LIMIT),
        name="fox_attn",
    )(q, kt, v, c_pairs)


def _ffn_kernel(*refs, has_proj, final_norm, chunks):
    refs = list(refs)
    x_ref = refs.pop(0)
    if has_proj:
        ya_ref, yr_ref, wo_ref = refs.pop(0), refs.pop(0), refs.pop(0)
    g_ref, win_ref, wout_ref = refs.pop(0), refs.pop(0), refs.pop(0)
    if final_norm:
        gf_ref = refs.pop(0)
    o_ref = refs.pop(0)

    x = x_ref[...]
    if has_proj:
        y = jnp.concatenate([ya_ref[...], yr_ref[...]], axis=1)
        x = x + jnp.dot(y, wo_ref[...], preferred_element_type=F32)
    hn = _rmsnorm(x, g_ref[...]).astype(BF16)
    d_ff = wout_ref.shape[0]
    acc = None
    for lo, hi in zip(chunks[:-1], chunks[1:]):
        gate = jnp.dot(hn, win_ref[:, lo:hi], preferred_element_type=F32)
        up = jnp.dot(hn, win_ref[:, d_ff + lo:d_ff + hi], preferred_element_type=F32)
        act = (gate * _sigmoid(gate) * up).astype(BF16)
        part = jnp.dot(act, wout_ref[lo:hi, :], preferred_element_type=F32)
        acc = part if acc is None else acc + part
    x = x + 0.5 * acc
    if final_norm:
        x = _rmsnorm(x, gf_ref[...])
    o_ref[...] = x


def _ffn(x, g, w_in, w_out, w_idx, *, proj=None, final_g=None, tm, chunks):
    t, d = x.shape
    const = lambda i: (0, 0)
    rows = lambda i: (i, 0)

    def resident(w, lead):
        block = (None,) * len(lead) + w.shape[len(lead):]
        return pl.BlockSpec(block, lambda i: lead + (0, 0), pipeline_mode=pl.Buffered(1))

    args = [x]
    in_specs = [pl.BlockSpec((tm, d), rows)]
    if proj is not None:
        ya, yr, wo = proj
        args += [ya, yr, wo]
        in_specs += [pl.BlockSpec((tm, ya.shape[1]), rows), pl.BlockSpec((tm, yr.shape[1]), rows),
                     resident(wo, w_idx[:1])]
    args += [g, w_in, w_out]
    in_specs += [pl.BlockSpec((1, d), const), resident(w_in, w_idx), resident(w_out, w_idx)]
    if final_g is not None:
        args.append(final_g)
        in_specs.append(pl.BlockSpec((1, d), const))
    kern = functools.partial(_ffn_kernel, has_proj=proj is not None,
                             final_norm=final_g is not None, chunks=chunks)
    return pl.pallas_call(
        kern,
        grid=(t // tm,),
        in_specs=in_specs,
        out_specs=pl.BlockSpec((tm, d), rows),
        out_shape=jax.ShapeDtypeStruct((t, d), F32),
        compiler_params=pltpu.CompilerParams(
            dimension_semantics=("parallel",), vmem_limit_bytes=VMEM_LIMIT),
        name="ffn",
    )(*args)


def _pick_tile(n, target):
    t = min(n, target)
    while n % t:
        t //= 2
    return t


def _ffn_chunks(d_ff):
    if d_ff % MXU_TILE or d_ff < 2 * MXU_TILE:
        return (0, d_ff)
    tiles = d_ff // MXU_TILE
    return (0, (tiles + 1) // 2 * MXU_TILE, d_ff)


def kernel(x, norm_g, w_in, b_f, conv_w, conv_b, w_rg_a, b_rg_a, w_rg_x, b_rg_x, rg_lambda,
           w_out, w_ffn_in, w_ffn_out, final_g):
    b, s, d = x.shape
    depth = norm_g.shape[0]
    n_heads = b_f.shape[-1]
    d_rec = conv_w.shape[-1]
    n_blocks = w_rg_a.shape[1]
    d_attn = (w_in.shape[-1] - n_heads - 2 * d_rec) // 3
    hd = d_attn // n_heads
    d_ff = w_ffn_out.shape[2]
    assert 2 * hd == LANES and n_heads % 2 == 0 and n_heads <= SUBLANES
    assert d_attn % LANES == 0 and d_rec % LANES == 0 and d % LANES == 0

    tm_mix = _pick_tile(s, 512)
    tq = _pick_tile(s, 512)
    tm_ffn = _pick_tile(b * s, 1024)
    chunks = _ffn_chunks(d_ff)

    o_k, o_v, o_f = d_attn, 2 * d_attn, 3 * d_attn
    o_xr = o_f + n_heads
    tri = jnp.concatenate([jnp.triu(jnp.ones((tm_mix, tm_mix), BF16)),
                           jnp.ones((tm_mix, LANES), BF16)], axis=1)
    eye = jnp.eye(n_blocks, dtype=F32)[:, None, :, None]

    def block_diag(w):
        return (w[:, :, None, :] * eye).reshape(d_rec, d_rec).astype(BF16)

    def mixer_weights(l):
        wl = w_in[l]
        w_rows = jnp.concatenate([wl[:, :o_k], wl[:, o_v:o_f], wl[:, o_xr:]], axis=-1).astype(BF16)
        w_f = jnp.pad(wl[:, o_f:o_xr], ((0, 0), (0, BF16_ROWS - n_heads)))
        w_cols = jnp.concatenate([wl[:, o_k:o_v], w_f], axis=-1).T.astype(BF16)
        bf_rep = jnp.broadcast_to(jnp.pad(b_f[l], (0, SUBLANES - n_heads))[:, None],
                                  (SUBLANES, LANES))
        return (w_rows, w_cols, bf_rep, tri, conv_w[l], conv_b[l][None, :],
                block_diag(w_rg_a[l]), b_rg_a[l][None, :], block_diag(w_rg_x[l]),
                b_rg_x[l][None, :], rg_lambda[l][None, :])

    w_out_b = w_out.astype(BF16)
    w_ffn_in_b = w_ffn_in.astype(BF16)
    w_ffn_out_b = w_ffn_out.astype(BF16)

    xf = x.reshape(b * s, d)
    for l in range(depth):
        g = norm_g[l][:, None, :]
        xf = _ffn(xf, g[0], w_ffn_in_b, w_ffn_out_b, (l, 0), tm=tm_ffn, chunks=chunks)
        q, kt, v, ct, y_rec = _mixer_in(
            xf.reshape(b, s, d), g[1], *mixer_weights(l),
            d_attn=d_attn, d_rec=d_rec, scale=LOG2E / math.sqrt(hd), tm=tm_mix)
        c_pairs = ct[:, :n_heads, :].reshape(b, n_heads // 2, 2, s)
        y_attn = _attention(q, kt, v, c_pairs, hd=hd, tq=tq)
        last = l == depth - 1
        xf = _ffn(xf, g[2], w_ffn_in_b, w_ffn_out_b, (l, 1),
                  proj=(y_attn.reshape(b * s, d_attn), y_rec.reshape(b * s, d_rec), w_out_b),
                  final_g=final_g[None, :] if last else None, tm=tm_ffn, chunks=chunks)
    return xf.reshape(b, s, d)
```

```python
import functools
import math

import jax
import jax.numpy as jnp
from jax import lax
from jax.experimental import pallas as pl
from jax.experimental.pallas import tpu as pltpu

EPS = 1e-6
RG_C = 8.0
LANES = 128
SUBLANES = 8
BF16_ROWS = 16
MXU_TILE = 256
NEG_BIG = -1e30
LOG2E = math.log2(math.e)
VMEM_LIMIT = 56 * 1024 * 1024

F32 = jnp.float32
BF16 = jnp.bfloat16
NT_DIMS = (((1,), (1,)), ((), ()))


def _rmsnorm(x, g):
    ms = jnp.mean(x * x, axis=-1, keepdims=True)
    return x * lax.rsqrt(ms + EPS) * g


def _sigmoid(x):
    return 1.0 / (1.0 + jnp.exp(-x))


def _softplus(x):
    return jnp.maximum(x, 0.0) + jnp.log1p(jnp.exp(-jnp.abs(x)))


def _gelu_tanh(x):
    c = math.sqrt(2.0 / math.pi)
    return (0.5 * x) * (1.0 + jnp.tanh(x * (c + (c * 0.044715) * (x * x))))


def _shift_rows(x, k, fill):
    rows = lax.broadcasted_iota(jnp.int32, x.shape, 0)
    return jnp.where(rows >= k, pltpu.roll(x, k, 0), fill)


def _tile_lanes(x, n):
    return jnp.concatenate([x] * n, axis=1)


def _split3(x):
    hi = x.astype(BF16).astype(F32)
    r = x - hi
    mid = r.astype(BF16).astype(F32)
    lo = (r - mid).astype(BF16).astype(F32)
    return hi, mid, lo


def _mixer_in_kernel(x_ref, g_ref, w_ref, wt_ref, bf_ref, tri_ref, cw_ref, cb_ref, wa_ref, ba_ref,
                     wx_ref, bx_ref, lam_ref,
                     q_ref, kt_ref, v_ref, ct_ref, yr_ref,
                     c_carry, h_carry, x_carry, a_s, u_s, hs_s, *, d_attn, d_rec, scale):
    j = pl.program_id(1)
    tm = x_ref.shape[1]
    nb = tm // SUBLANES

    @pl.when(j == 0)
    def _():
        c_carry[...] = jnp.zeros_like(c_carry)
        h_carry[...] = jnp.zeros_like(h_carry)
        x_carry[...] = jnp.zeros_like(x_carry)

    hn = _rmsnorm(x_ref[0], g_ref[...]).astype(BF16)
    z = jnp.dot(hn, w_ref[...], preferred_element_type=F32)
    zt = lax.dot_general(wt_ref[...], hn, NT_DIMS, preferred_element_type=F32)

    q_ref[0] = (z[:, :d_attn] * scale).astype(BF16)
    v_ref[0] = z[:, d_attn:2 * d_attn].astype(BF16)
    kt_ref[0] = zt[:d_attn].astype(BF16)
    o_xr = 2 * d_attn
    o_gr = o_xr + d_rec
    xr = z[:, o_xr:o_gr]
    gr = z[:, o_gr:]

    f_logit = zt[d_attn:d_attn + SUBLANES] + _tile_lanes(bf_ref[...], tm // LANES)
    log_f = -_softplus(-f_logit)
    parts = jnp.concatenate(_split3(log_f) + (jnp.zeros_like(log_f),), axis=0).astype(BF16)
    cs = jnp.dot(parts, tri_ref[...], preferred_element_type=F32)
    cs = cs[0:SUBLANES] + cs[SUBLANES:2 * SUBLANES] + cs[2 * SUBLANES:3 * SUBLANES]
    carry = c_carry[...]
    ct_ref[0] = cs[:, :tm] + _tile_lanes(carry, tm // LANES)
    c_carry[...] = carry + cs[:, tm:]

    width = cw_ref.shape[0]
    ext = jnp.concatenate([x_carry[...], xr], axis=0)
    x_carry[...] = xr[tm - SUBLANES:, :]
    xc = xr * cw_ref[width - 1:width, :] + cb_ref[...]
    for s in range(1, width):
        xc = xc + pltpu.roll(ext, s, 0)[SUBLANES:, :] * cw_ref[width - 1 - s:width - s, :]

    xcb = xc.astype(BF16)
    r = _sigmoid(jnp.dot(xcb, wa_ref[...], preferred_element_type=F32) + ba_ref[...])
    i = _sigmoid(jnp.dot(xcb, wx_ref[...], preferred_element_type=F32) + bx_ref[...])
    log_a = (-RG_C * _softplus(-lam_ref[...])) * r
    a = jnp.exp(log_a)
    u = jnp.sqrt(-jnp.tanh(log_a) * (1.0 + a * a)) * (i * xc)

    a3 = a.reshape(nb, SUBLANES, d_rec)
    u3 = u.reshape(nb, SUBLANES, d_rec)
    sub = lax.broadcasted_iota(jnp.int32, a3.shape, 1)
    k = 1
    while k < SUBLANES:
        keep = sub >= k
        u3 = u3 + a3 * jnp.where(keep, pltpu.roll(u3, k, 1), 0.0)
        a3 = a3 * jnp.where(keep, pltpu.roll(a3, k, 1), 1.0)
        k *= 2
    a2 = a3.reshape(tm, d_rec)
    u2 = u3.reshape(tm, d_rec)
    n_lt = d_rec // LANES
    for c in range(n_lt):
        a_s[c] = a2[:, c * LANES:(c + 1) * LANES]
        u_s[c] = u2[:, c * LANES:(c + 1) * LANES]
    last_rows = pl.ds(SUBLANES - 1, nb, stride=SUBLANES)
    ab = jnp.concatenate([a_s[c, last_rows, :] for c in range(n_lt)], axis=1)
    ub = jnp.concatenate([u_s[c, last_rows, :] for c in range(n_lt)], axis=1)
    k = 1
    while k < nb:
        ub = ub + ab * _shift_rows(ub, k, 0.0)
        ab = ab * _shift_rows(ab, k, 1.0)
        k *= 2
    h_in = h_carry[...]
    h_end = ab * h_in + ub
    h_carry[...] = h_end[nb - 1:nb, :]
    hs_s[...] = _shift_rows(h_end, 1, h_in)
    for b in range(nb):
        rows = slice(b * SUBLANES, (b + 1) * SUBLANES)
        for c in range(n_lt):
            u_s[c, rows, :] = (a_s[c, rows, :] * hs_s[b:b + 1, c * LANES:(c + 1) * LANES]
                               + u_s[c, rows, :])
    h = jnp.concatenate([u_s[c] for c in range(n_lt)], axis=1)

    yr_ref[0] = (h * _gelu_tanh(gr)).astype(BF16)


def _mixer_in(x, g, w, wt, bf_rep, tri, conv_w, conv_b, wa, ba, wx, bx, lam, *, d_attn, d_rec, scale, tm):
    b, s, d = x.shape
    const = lambda *_: (0, 0)
    row3 = lambda bi, j: (bi, j, 0)
    col3 = lambda bi, j: (bi, 0, j)
    kern = functools.partial(_mixer_in_kernel, d_attn=d_attn, d_rec=d_rec, scale=scale)
    return pl.pallas_call(
        kern,
        grid=(b, s // tm),
        in_specs=[
            pl.BlockSpec((1, tm, d), row3),
            pl.BlockSpec((1, d), const),
            pl.BlockSpec(w.shape, const),
            pl.BlockSpec(wt.shape, const),
            pl.BlockSpec((SUBLANES, LANES), const),
            pl.BlockSpec(tri.shape, const),
            pl.BlockSpec(conv_w.shape, const),
            pl.BlockSpec((1, d_rec), const),
            pl.BlockSpec((d_rec, d_rec), const),
            pl.BlockSpec((1, d_rec), const),
            pl.BlockSpec((d_rec, d_rec), const),
            pl.BlockSpec((1, d_rec), const),
            pl.BlockSpec((1, d_rec), const),
        ],
        out_specs=[
            pl.BlockSpec((1, tm, d_attn), row3),
            pl.BlockSpec((1, d_attn, tm), col3),
            pl.BlockSpec((1, tm, d_attn), row3),
            pl.BlockSpec((1, SUBLANES, tm), col3),
            pl.BlockSpec((1, tm, d_rec), row3),
        ],
        out_shape=[
            jax.ShapeDtypeStruct((b, s, d_attn), BF16),
            jax.ShapeDtypeStruct((b, d_attn, s), BF16),
            jax.ShapeDtypeStruct((b, s, d_attn), BF16),
            jax.ShapeDtypeStruct((b, SUBLANES, s), F32),
            jax.ShapeDtypeStruct((b, s, d_rec), BF16),
        ],
        scratch_shapes=[
            pltpu.VMEM((SUBLANES, LANES), F32),
            pltpu.VMEM((1, d_rec), F32),
            pltpu.VMEM((SUBLANES, d_rec), F32),
            pltpu.VMEM((d_rec // LANES, tm, LANES), F32),
            pltpu.VMEM((d_rec // LANES, tm, LANES), F32),
            pltpu.VMEM((tm // SUBLANES, d_rec), F32),
        ],
        compiler_params=pltpu.CompilerParams(
            dimension_semantics=("parallel", "arbitrary"), vmem_limit_bytes=VMEM_LIMIT),
        name="mixer_in",
    )(x, g, w, wt, bf_rep, tri, conv_w, conv_b, wa, ba, wx, bx, lam)


def _attn_kernel(q_ref, kt_ref, v_ref, c_ref, o_ref, ka_ref, kb_ref, vbd_ref, acc_ref, m_ref,
                 se_ref, so_ref, *, tq, hd):
    s_len = q_ref.shape[1]
    nq = s_len // tq
    half = tq // 2
    lane = lax.broadcasted_iota(jnp.int32, (1, LANES), 1)
    is_a = lane < hd

    sub = lax.broadcasted_iota(jnp.int32, (BF16_ROWS, s_len), 0)

    def bias_rows(h):
        hi, mid, lo = _split3(c_ref[0, 0, h:h + 1, :] * (-LOG2E))
        rows = jnp.where(sub == 0, hi, jnp.where(sub == 1, mid, jnp.where(sub == 2, lo, 0.0)))
        return rows.astype(BF16)

    zeros_k = jnp.zeros((hd - BF16_ROWS, s_len), BF16)
    ka_ref[0:hd, :] = kt_ref[0, 0:hd, :]
    ka_ref[hd:hd + BF16_ROWS, :] = bias_rows(0)
    ka_ref[hd + BF16_ROWS:, :] = zeros_k
    kb_ref[0:BF16_ROWS, :] = bias_rows(1)
    kb_ref[BF16_ROWS:hd, :] = zeros_k
    kb_ref[hd:, :] = kt_ref[0, hd:, :]
    k_aug = (ka_ref, kb_ref)
    n_bias = 3
    ones_a = jnp.where((lane >= hd) & (lane < hd + n_bias), 1.0, 0.0).astype(BF16)
    ones_b = jnp.where(lane < n_bias, 1.0, 0.0).astype(BF16)

    zeros_v = jnp.zeros((half, LANES), BF16)
    ones_v = jnp.ones((half, LANES), BF16)
    for jb in range(s_len // half):
        v2 = v_ref[0, jb * half:(jb + 1) * half, :]
        vbd_ref[jb, 0:half, 0:LANES] = jnp.where(is_a, v2, ones_v)
        vbd_ref[jb, 0:half, LANES:] = zeros_v
        vbd_ref[jb, half:, 0:LANES] = zeros_v
        vbd_ref[jb, half:, LANES:] = jnp.where(is_a, ones_v, v2)

    row_t = lax.broadcasted_iota(jnp.int32, (half, half), 0)
    col_t = lax.broadcasted_iota(jnp.int32, (half, half), 1)
    mask_top = col_t <= row_t
    row_b = lax.broadcasted_iota(jnp.int32, (half, tq), 0)
    col_b = lax.broadcasted_iota(jnp.int32, (half, tq), 1)
    mask_bot = col_b <= row_b + half

    def q_tile(qi, first_buf):
        q0 = qi * tq
        q2 = q_ref[0, q0:q0 + tq, :]
        q_heads = (jnp.where(is_a, q2, ones_a), jnp.where(is_a, ones_b, q2))
        bufs = (se_ref, so_ref) if first_buf == 0 else (so_ref, se_ref)
        acc = acc_ref.at[qi % 2]
        m_run = m_ref.at[qi % 2]
        m_run[...] = jnp.full(m_run.shape, NEG_BIG, F32)
        acc[...] = jnp.zeros(acc.shape, F32)

        def qk(j, dst):
            for h in range(2):
                dst[h] = jnp.dot(q_heads[h], k_aug[h][:, j * tq:(j + 1) * tq],
                                 preferred_element_type=F32)

        def softmax_pv(s_heads, r0, jb0, mask):
            nr, tk = s_heads[0].shape
            nblk = tk // half
            ps, alphas = [], []
            for h in range(2):
                s = s_heads[h]
                if mask is not None:
                    s = jnp.where(mask, s, NEG_BIG)
                m_old = m_run[h, r0:r0 + nr]
                m_new = jnp.maximum(m_old, jnp.max(s, axis=1, keepdims=True))
                alphas.append(jnp.exp2(m_old - m_new))
                ps.append(jnp.exp2((s - _tile_lanes(m_new, tk // LANES)).astype(BF16)))
                m_run[h, r0:r0 + nr] = m_new
            p_cat = jnp.concatenate([ps[h][:, i * half:(i + 1) * half]
                                     for i in range(nblk) for h in range(2)], axis=1)
            v_blk = vbd_ref[jb0:jb0 + nblk].reshape(nblk * 2 * half, 2 * LANES)
            pv = jnp.dot(p_cat, v_blk, preferred_element_type=F32)
            acc[r0:r0 + nr] = jnp.concatenate(alphas, axis=1) * acc[r0:r0 + nr] + pv

        qk(0, bufs[0])
        for j in range(qi):
            src, dst = bufs[j % 2], bufs[(j + 1) % 2]
            s_heads = (src[0], src[1])
            qk(j + 1, dst)
            softmax_pv(s_heads, 0, 2 * j, None)
        src = bufs[qi % 2]
        softmax_pv((src[0, 0:half, 0:half], src[1, 0:half, 0:half]), 0, 2 * qi, mask_top)
        softmax_pv((src[0, half:, :], src[1, half:, :]), half, 2 * qi, mask_bot)

        acc_a = acc[:, :LANES]
        acc_b = acc[:, LANES:]
        num = jnp.where(is_a, acc_a, acc_b)
        den = pltpu.roll(jnp.where(is_a, acc_b, acc_a), hd, 1)
        o_ref[0, q0:q0 + tq, :] = (num / den).astype(BF16)

    first_buf = 0
    for qi in range(nq):
        q_tile(qi, first_buf)
        first_buf = 1 - (first_buf + qi) % 2


def _attention(q, kt, v, c_pairs, *, hd, tq):
    b, s, d_attn = q.shape
    n_pairs = d_attn // LANES
    blk = lambda bi, p: (bi, 0, p)
    kern = functools.partial(_attn_kernel, tq=tq, hd=hd)
    return pl.pallas_call(
        kern,
        grid=(b, n_pairs),
        in_specs=[
            pl.BlockSpec((1, s, LANES), blk),
            pl.BlockSpec((1, LANES, s), lambda bi, p: (bi, p, 0)),
            pl.BlockSpec((1, s, LANES), blk),
            pl.BlockSpec((1, 1, 2, s), lambda bi, p: (bi, p, 0, 0)),
        ],
        out_specs=pl.BlockSpec((1, s, LANES), blk),
        out_shape=jax.ShapeDtypeStruct((b, s, d_attn), BF16),
        scratch_shapes=[
            pltpu.VMEM((LANES, s), BF16),
            pltpu.VMEM((LANES, s), BF16),
            pltpu.VMEM((2 * s // tq, tq, 2 * LANES), BF16),
            pltpu.VMEM((2, tq, 2 * LANES), F32),
            pltpu.VMEM((2, 2, tq, LANES), F32),
            pltpu.VMEM((2, tq, tq), F32),
            pltpu.VMEM((2, tq, tq), F32),
        ],
        compiler_params=pltpu.CompilerParams(
            dimension_semantics=("parallel", "parallel"), vmem_limit_bytes=VMEM_LIMIT),
        name="fox_attn",
    )(q, kt, v, c_pairs)


def _ffn_kernel(*refs, has_proj, final_norm, chunks):
    refs = list(refs)
    x_ref = refs.pop(0)
    if has_proj:
        ya_ref, yr_ref, wo_ref = refs.pop(0), refs.pop(0), refs.pop(0)
    g_ref, win_ref, wout_ref = refs.pop(0), refs.pop(0), refs.pop(0)
    if final_norm:
        gf_ref = refs.pop(0)
    o_ref = refs.pop(0)

    x = x_ref[...]
    if has_proj:
        y = jnp.concatenate([ya_ref[...], yr_ref[...]], axis=1)
        x = x + jnp.dot(y, wo_ref[...], preferred_element_type=F32)
    hn = _rmsnorm(x, g_ref[...]).astype(BF16)
    d_ff = wout_ref.shape[0]
    acc = None
    for lo, hi in zip(chunks[:-1], chunks[1:]):
        gate = jnp.dot(hn, win_ref[:, lo:hi], preferred_element_type=F32)
        up = jnp.dot(hn, win_ref[:, d_ff + lo:d_ff + hi], preferred_element_type=F32)
        act = (gate * _sigmoid(gate) * up).astype(BF16)
        part = jnp.dot(act, wout_ref[lo:hi, :], preferred_element_type=F32)
        acc = part if acc is None else acc + part
    x = x + 0.5 * acc
    if final_norm:
        x = _rmsnorm(x, gf_ref[...])
    o_ref[...] = x


def _ffn(x, g, w_in, w_out, w_idx, *, proj=None, final_g=None, tm, chunks):
    t, d = x.shape
    const = lambda i: (0, 0)
    rows = lambda i: (i, 0)

    def resident(w, lead):
        block = (None,) * len(lead) + w.shape[len(lead):]
        return pl.BlockSpec(block, lambda i: lead + (0, 0), pipeline_mode=pl.Buffered(1))

    args = [x]
    in_specs = [pl.BlockSpec((tm, d), rows)]
    if proj is not None:
        ya, yr, wo = proj
        args += [ya, yr, wo]
        in_specs += [pl.BlockSpec((tm, ya.shape[1]), rows), pl.BlockSpec((tm, yr.shape[1]), rows),
                     resident(wo, w_idx[:1])]
    args += [g, w_in, w_out]
    in_specs += [pl.BlockSpec((1, d), const), resident(w_in, w_idx), resident(w_out, w_idx)]
    if final_g is not None:
        args.append(final_g)
        in_specs.append(pl.BlockSpec((1, d), const))
    kern = functools.partial(_ffn_kernel, has_proj=proj is not None,
                             final_norm=final_g is not None, chunks=chunks)
    return pl.pallas_call(
        kern,
        grid=(t // tm,),
        in_specs=in_specs,
        out_specs=pl.BlockSpec((tm, d), rows),
        out_shape=jax.ShapeDtypeStruct((t, d), F32),
        compiler_params=pltpu.CompilerParams(
            dimension_semantics=("parallel",), vmem_limit_bytes=VMEM_LIMIT),
        name="ffn",
    )(*args)


def _pick_tile(n, target):
    t = min(n, target)
    while n % t:
        t //= 2
    return t


def _ffn_chunks(d_ff):
    if d_ff % MXU_TILE or d_ff < 2 * MXU_TILE:
        return (0, d_ff)
    tiles = d_ff // MXU_TILE
    return (0, (tiles + 1) // 2 * MXU_TILE, d_ff)


def kernel(x, norm_g, w_in, b_f, conv_w, conv_b, w_rg_a, b_rg_a, w_rg_x, b_rg_x, rg_lambda,
           w_out, w_ffn_in, w_ffn_out, final_g):
    b, s, d = x.shape
    depth = norm_g.shape[0]
    n_heads = b_f.shape[-1]
    d_rec = conv_w.shape[-1]
    n_blocks = w_rg_a.shape[1]
    d_attn = (w_in.shape[-1] - n_heads - 2 * d_rec) // 3
    hd = d_attn // n_heads
    d_ff = w_ffn_out.shape[2]
    assert 2 * hd == LANES and n_heads % 2 == 0 and n_heads <= SUBLANES
    assert d_attn % LANES == 0 and d_rec % LANES == 0 and d % LANES == 0

    tm_mix = _pick_tile(s, 512)
    tq = _pick_tile(s, 512)
    tm_ffn = _pick_tile(b * s, 1024)
    chunks = _ffn_chunks(d_ff)

    o_k, o_v, o_f = d_attn, 2 * d_attn, 3 * d_attn
    o_xr = o_f + n_heads
    tri = jnp.concatenate([jnp.triu(jnp.ones((tm_mix, tm_mix), BF16)),
                           jnp.ones((tm_mix, LANES), BF16)], axis=1)
    eye = jnp.eye(n_blocks, dtype=F32)[:, None, :, None]

    def block_diag(w):
        return (w[:, :, None, :] * eye).reshape(d_rec, d_rec).astype(BF16)

    def mixer_weights(l):
        wl = w_in[l]
        w_rows = jnp.concatenate([wl[:, :o_k], wl[:, o_v:o_f], wl[:, o_xr:]], axis=-1).astype(BF16)
        w_f = jnp.pad(wl[:, o_f:o_xr], ((0, 0), (0, BF16_ROWS - n_heads)))
        w_cols = jnp.concatenate([wl[:, o_k:o_v], w_f], axis=-1).T.astype(BF16)
        bf_rep = jnp.broadcast_to(jnp.pad(b_f[l], (0, SUBLANES - n_heads))[:, None],
                                  (SUBLANES, LANES))
        return (w_rows, w_cols, bf_rep, tri, conv_w[l], conv_b[l][None, :],
                block_diag(w_rg_a[l]), b_rg_a[l][None, :], block_diag(w_rg_x[l]),
                b_rg_x[l][None, :], rg_lambda[l][None, :])

    w_out_b = w_out.astype(BF16)
    w_ffn_in_b = w_ffn_in.astype(BF16)
    w_ffn_out_b = w_ffn_out.astype(BF16)

    xf = x.reshape(b * s, d)
    for l in range(depth):
        g = norm_g[l][:, None, :]
        xf = _ffn(xf, g[0], w_ffn_in_b, w_ffn_out_b, (l, 0), tm=tm_ffn, chunks=chunks)
        q, kt, v, ct, y_rec = _mixer_in(
            xf.reshape(b, s, d), g[1], *mixer_weights(l),
            d_attn=d_attn, d_rec=d_rec, scale=LOG2E / math.sqrt(hd), tm=tm_mix)
        c_pairs = ct[:, :n_heads, :].reshape(b, n_heads // 2, 2, s)
        y_attn = _attention(q, kt, v, c_pairs, hd=hd, tq=tq)
        last = l == depth - 1
        xf = _ffn(xf, g[2], w_ffn_in_b, w_ffn_out_b, (l, 1),
                  proj=(y_attn.reshape(b * s, d_attn), y_rec.reshape(b * s, d_rec), w_out_b),
                  final_g=final_g[None, :] if last else None, tm=tm_ffn, chunks=chunks)
    return xf.reshape(b, s, d)
```

```python
import functools
import math

import jax
import jax.numpy as jnp
from jax import lax
from jax.experimental import pallas as pl
from jax.experimental.pallas import tpu as pltpu

EPS = 1e-6
RG_C = 8.0
LANES = 128
SUBLANES = 8
BF16_ROWS = 16
MXU_TILE = 256
N_SPLIT = 3
NEG_BIG = -1e30
LOG2E = math.log2(math.e)
VMEM_LIMIT = 56 * 1024 * 1024

F32 = jnp.float32
BF16 = jnp.bfloat16
NT_DIMS = (((1,), (1,)), ((), ()))


def _rmsnorm(x, g):
    ms = jnp.mean(x * x, axis=-1, keepdims=True)
    return x * lax.rsqrt(ms + EPS) * g


def _sigmoid(x):
    return 1.0 / (1.0 + jnp.exp(-x))


def _softplus(x):
    return jnp.maximum(x, 0.0) + jnp.log1p(jnp.exp(-jnp.abs(x)))


def _gelu_tanh(x):
    c = math.sqrt(2.0 / math.pi)
    return (0.5 * x) * (1.0 + jnp.tanh(x * (c + (c * 0.044715) * (x * x))))


def _shift_rows(x, k, fill):
    rows = lax.broadcasted_iota(jnp.int32, x.shape, 0)
    return jnp.where(rows >= k, pltpu.roll(x, k, 0), fill)


def _tile_lanes(x, n):
    return jnp.concatenate([x] * n, axis=1)


def _split3(x):
    hi = x.astype(BF16).astype(F32)
    r = x - hi
    mid = r.astype(BF16).astype(F32)
    lo = (r - mid).astype(BF16).astype(F32)
    return hi, mid, lo


def _mixer_in_kernel(x_ref, g_ref, w_ref, wt_ref, bf_ref, tri_ref, cw_ref, cb_ref, wa_ref, ba_ref,
                     wx_ref, bx_ref, lam_ref,
                     q_ref, kt_ref, v_ref, ct_ref, yr_ref,
                     c_carry, h_carry, x_carry, a_s, u_s, hs_s, *, d_attn, d_rec, scale):
    j = pl.program_id(1)
    tm = x_ref.shape[1]
    nb = tm // SUBLANES

    @pl.when(j == 0)
    def _():
        c_carry[...] = jnp.zeros_like(c_carry)
        h_carry[...] = jnp.zeros_like(h_carry)
        x_carry[...] = jnp.zeros_like(x_carry)

    hn = _rmsnorm(x_ref[0], g_ref[...]).astype(BF16)
    z = jnp.dot(hn, w_ref[...], preferred_element_type=F32)
    zt = lax.dot_general(wt_ref[...], hn, NT_DIMS, preferred_element_type=F32)

    q_ref[0] = (z[:, :d_attn] * scale).astype(BF16)
    v_ref[0] = z[:, d_attn:2 * d_attn].astype(BF16)
    kt_ref[0] = zt[:d_attn].astype(BF16)
    o_xr = 2 * d_attn
    o_gr = o_xr + d_rec
    xr = z[:, o_xr:o_gr]
    gr = z[:, o_gr:]

    f_logit = zt[d_attn:d_attn + SUBLANES] + _tile_lanes(bf_ref[...], tm // LANES)
    log_f = -_softplus(-f_logit)
    parts = jnp.concatenate(_split3(log_f) + (jnp.zeros_like(log_f),), axis=0).astype(BF16)
    cs = jnp.dot(parts, tri_ref[...], preferred_element_type=F32)
    cs = cs[0:SUBLANES] + cs[SUBLANES:2 * SUBLANES] + cs[2 * SUBLANES:3 * SUBLANES]
    carry = c_carry[...]
    ct_ref[0] = cs[:, :tm] + _tile_lanes(carry, tm // LANES)
    c_carry[...] = carry + cs[:, tm:]

    width = cw_ref.shape[0]
    ext = jnp.concatenate([x_carry[...], xr], axis=0)
    x_carry[...] = xr[tm - SUBLANES:, :]
    xc = xr * cw_ref[width - 1:width, :] + cb_ref[...]
    for s in range(1, width):
        xc = xc + pltpu.roll(ext, s, 0)[SUBLANES:, :] * cw_ref[width - 1 - s:width - s, :]

    xcb = xc.astype(BF16)
    r = _sigmoid(jnp.dot(xcb, wa_ref[...], preferred_element_type=F32) + ba_ref[...])
    i = _sigmoid(jnp.dot(xcb, wx_ref[...], preferred_element_type=F32) + bx_ref[...])
    log_a = (-RG_C * _softplus(-lam_ref[...])) * r
    a = jnp.exp(log_a)
    u = jnp.sqrt(-jnp.tanh(log_a) * (1.0 + a * a)) * (i * xc)

    a3 = a.reshape(nb, SUBLANES, d_rec)
    u3 = u.reshape(nb, SUBLANES, d_rec)
    sub = lax.broadcasted_iota(jnp.int32, a3.shape, 1)
    k = 1
    while k < SUBLANES:
        keep = sub >= k
        u3 = u3 + a3 * jnp.where(keep, pltpu.roll(u3, k, 1), 0.0)
        a3 = a3 * jnp.where(keep, pltpu.roll(a3, k, 1), 1.0)
        k *= 2
    a2 = a3.reshape(tm, d_rec)
    u2 = u3.reshape(tm, d_rec)
    n_lt = d_rec // LANES
    for c in range(n_lt):
        a_s[c] = a2[:, c * LANES:(c + 1) * LANES]
        u_s[c] = u2[:, c * LANES:(c + 1) * LANES]
    last_rows = pl.ds(SUBLANES - 1, nb, stride=SUBLANES)
    ab = jnp.concatenate([a_s[c, last_rows, :] for c in range(n_lt)], axis=1)
    ub = jnp.concatenate([u_s[c, last_rows, :] for c in range(n_lt)], axis=1)
    k = 1
    while k < nb:
        ub = ub + ab * _shift_rows(ub, k, 0.0)
        ab = ab * _shift_rows(ab, k, 1.0)
        k *= 2
    h_in = h_carry[...]
    h_end = ab * h_in + ub
    h_carry[...] = h_end[nb - 1:nb, :]
    hs_s[...] = _shift_rows(h_end, 1, h_in)
    for b in range(nb):
        rows = slice(b * SUBLANES, (b + 1) * SUBLANES)
        for c in range(n_lt):
            u_s[c, rows, :] = (a_s[c, rows, :] * hs_s[b:b + 1, c * LANES:(c + 1) * LANES]
                               + u_s[c, rows, :])
    h = jnp.concatenate([u_s[c] for c in range(n_lt)], axis=1)

    yr_ref[0] = (h * _gelu_tanh(gr)).astype(BF16)


def _mixer_in(x, g, w, wt, bf_rep, tri, conv_w, conv_b, wa, ba, wx, bx, lam, *, d_attn, d_rec, scale, tm):
    b, s, d = x.shape
    const = lambda *_: (0, 0)
    row3 = lambda bi, j: (bi, j, 0)
    col3 = lambda bi, j: (bi, 0, j)
    kern = functools.partial(_mixer_in_kernel, d_attn=d_attn, d_rec=d_rec, scale=scale)
    return pl.pallas_call(
        kern,
        grid=(b, s // tm),
        in_specs=[
            pl.BlockSpec((1, tm, d), row3),
            pl.BlockSpec((1, d), const),
            pl.BlockSpec(w.shape, const),
            pl.BlockSpec(wt.shape, const),
            pl.BlockSpec((SUBLANES, LANES), const),
            pl.BlockSpec(tri.shape, const),
            pl.BlockSpec(conv_w.shape, const),
            pl.BlockSpec((1, d_rec), const),
            pl.BlockSpec((d_rec, d_rec), const),
            pl.BlockSpec((1, d_rec), const),
            pl.BlockSpec((d_rec, d_rec), const),
            pl.BlockSpec((1, d_rec), const),
            pl.BlockSpec((1, d_rec), const),
        ],
        out_specs=[
            pl.BlockSpec((1, tm, d_attn), row3),
            pl.BlockSpec((1, d_attn, tm), col3),
            pl.BlockSpec((1, tm, d_attn), row3),
            pl.BlockSpec((1, SUBLANES, tm), col3),
            pl.BlockSpec((1, tm, d_rec), row3),
        ],
        out_shape=[
            jax.ShapeDtypeStruct((b, s, d_attn), BF16),
            jax.ShapeDtypeStruct((b, d_attn, s), BF16),
            jax.ShapeDtypeStruct((b, s, d_attn), BF16),
            jax.ShapeDtypeStruct((b, SUBLANES, s), F32),
            jax.ShapeDtypeStruct((b, s, d_rec), BF16),
        ],
        scratch_shapes=[
            pltpu.VMEM((SUBLANES, LANES), F32),
            pltpu.VMEM((1, d_rec), F32),
            pltpu.VMEM((SUBLANES, d_rec), F32),
            pltpu.VMEM((d_rec // LANES, tm, LANES), F32),
            pltpu.VMEM((d_rec // LANES, tm, LANES), F32),
            pltpu.VMEM((tm // SUBLANES, d_rec), F32),
        ],
        compiler_params=pltpu.CompilerParams(
            dimension_semantics=("parallel", "arbitrary"), vmem_limit_bytes=VMEM_LIMIT),
        name="mixer_in",
    )(x, g, w, wt, bf_rep, tri, conv_w, conv_b, wa, ba, wx, bx, lam)


def _attn_kernel(q_ref, kt_ref, v_ref, c_ref, o_ref, ka_ref, kb_ref, vbd_ref, acc_ref, m_ref,
                 se_ref, so_ref, *, tq, hd):
    s_len = q_ref.shape[1]
    nq = s_len // tq
    half = tq // 2
    lane = lax.broadcasted_iota(jnp.int32, (1, LANES), 1)
    is_a = lane < hd

    sub = lax.broadcasted_iota(jnp.int32, (BF16_ROWS, s_len), 0)

    def bias_rows(h):
        hi, mid, lo = _split3(c_ref[0, 0, h:h + 1, :] * (-LOG2E))
        rows = jnp.where(sub == 0, hi, jnp.where(sub == 1, mid, jnp.where(sub == 2, lo, 0.0)))
        return rows.astype(BF16)

    zeros_k = jnp.zeros((hd - BF16_ROWS, s_len), BF16)
    ka_ref[0:hd, :] = kt_ref[0, 0:hd, :]
    ka_ref[hd:hd + BF16_ROWS, :] = bias_rows(0)
    ka_ref[hd + BF16_ROWS:, :] = zeros_k
    kb_ref[0:BF16_ROWS, :] = bias_rows(1)
    kb_ref[BF16_ROWS:hd, :] = zeros_k
    kb_ref[hd:, :] = kt_ref[0, hd:, :]
    k_aug = (ka_ref, kb_ref)
    ones_a = jnp.where((lane >= hd) & (lane < hd + N_SPLIT), 1.0, 0.0).astype(BF16)
    ones_b = jnp.where(lane < N_SPLIT, 1.0, 0.0).astype(BF16)

    zeros_v = jnp.zeros((half, LANES), BF16)
    ones_v = jnp.ones((half, LANES), BF16)
    for jb in range(s_len // half):
        v2 = v_ref[0, jb * half:(jb + 1) * half, :]
        vbd_ref[jb, 0:half, 0:LANES] = jnp.where(is_a, v2, ones_v)
        vbd_ref[jb, 0:half, LANES:] = zeros_v
        vbd_ref[jb, half:, 0:LANES] = zeros_v
        vbd_ref[jb, half:, LANES:] = jnp.where(is_a, ones_v, v2)

    row_t = lax.broadcasted_iota(jnp.int32, (half, half), 0)
    col_t = lax.broadcasted_iota(jnp.int32, (half, half), 1)
    mask_top = col_t <= row_t
    row_b = lax.broadcasted_iota(jnp.int32, (half, tq), 0)
    col_b = lax.broadcasted_iota(jnp.int32, (half, tq), 1)
    mask_bot = col_b <= row_b + half

    def q_tile(qi, first_buf):
        q0 = qi * tq
        q2 = q_ref[0, q0:q0 + tq, :]
        q_heads = (jnp.where(is_a, q2, ones_a), jnp.where(is_a, ones_b, q2))
        bufs = (se_ref, so_ref) if first_buf == 0 else (so_ref, se_ref)
        acc = acc_ref.at[qi % 2]
        m_run = m_ref.at[qi % 2]
        m_run[...] = jnp.full(m_run.shape, NEG_BIG, F32)
        acc[...] = jnp.zeros(acc.shape, F32)

        def qk(j, dst):
            for h in range(2):
                dst[h] = jnp.dot(q_heads[h], k_aug[h][:, j * tq:(j + 1) * tq],
                                 preferred_element_type=F32)

        def softmax_pv(s_heads, r0, jb0, mask):
            nr, tk = s_heads[0].shape
            nblk = tk // half
            ps, alphas = [], []
            for h in range(2):
                s = s_heads[h]
                if mask is not None:
                    s = jnp.where(mask, s, NEG_BIG)
                m_old = m_run[h, r0:r0 + nr]
                m_new = jnp.maximum(m_old, jnp.max(s, axis=1, keepdims=True))
                alphas.append(jnp.exp2(m_old - m_new))
                ps.append(jnp.exp2((s - _tile_lanes(m_new, tk // LANES)).astype(BF16)))
                m_run[h, r0:r0 + nr] = m_new
            p_cat = jnp.concatenate([ps[h][:, i * half:(i + 1) * half]
                                     for i in range(nblk) for h in range(2)], axis=1)
            v_blk = vbd_ref[jb0:jb0 + nblk].reshape(nblk * 2 * half, 2 * LANES)
            pv = jnp.dot(p_cat, v_blk, preferred_element_type=F32)
            acc[r0:r0 + nr] = jnp.concatenate(alphas, axis=1) * acc[r0:r0 + nr] + pv

        qk(0, bufs[0])
        for j in range(qi):
            src, dst = bufs[j % 2], bufs[(j + 1) % 2]
            s_heads = (src[0], src[1])
            qk(j + 1, dst)
            softmax_pv(s_heads, 0, 2 * j, None)
        src = bufs[qi % 2]
        softmax_pv((src[0, 0:half, 0:half], src[1, 0:half, 0:half]), 0, 2 * qi, mask_top)
        softmax_pv((src[0, half:, :], src[1, half:, :]), half, 2 * qi, mask_bot)

        acc_a = acc[:, :LANES]
        acc_b = acc[:, LANES:]
        num = jnp.where(is_a, acc_a, acc_b)
        den = pltpu.roll(jnp.where(is_a, acc_b, acc_a), hd, 1)
        o_ref[0, q0:q0 + tq, :] = (num / den).astype(BF16)

    first_buf = 0
    for qi in range(nq):
        q_tile(qi, first_buf)
        first_buf = 1 - (first_buf + qi) % 2


def _attention(q, kt, v, c_pairs, *, hd, tq):
    b, s, d_attn = q.shape
    n_pairs = d_attn // LANES
    blk = lambda bi, p: (bi, 0, p)
    kern = functools.partial(_attn_kernel, tq=tq, hd=hd)
    return pl.pallas_call(
        kern,
        grid=(b, n_pairs),
        in_specs=[
            pl.BlockSpec((1, s, LANES), blk),
            pl.BlockSpec((1, LANES, s), lambda bi, p: (bi, p, 0)),
            pl.BlockSpec((1, s, LANES), blk),
            pl.BlockSpec((1, 1, 2, s), lambda bi, p: (bi, p, 0, 0)),
        ],
        out_specs=pl.BlockSpec((1, s, LANES), blk),
        out_shape=jax.ShapeDtypeStruct((b, s, d_attn), BF16),
        scratch_shapes=[
            pltpu.VMEM((LANES, s), BF16),
            pltpu.VMEM((LANES, s), BF16),
            pltpu.VMEM((2 * s // tq, tq, 2 * LANES), BF16),
            pltpu.VMEM((2, tq, 2 * LANES), F32),
            pltpu.VMEM((2, 2, tq, LANES), F32),
            pltpu.VMEM((2, tq, tq), F32),
            pltpu.VMEM((2, tq, tq), F32),
        ],
        compiler_params=pltpu.CompilerParams(
            dimension_semantics=("parallel", "parallel"), vmem_limit_bytes=VMEM_LIMIT),
        name="fox_attn",
    )(q, kt, v, c_pairs)


def _ffn_kernel(*refs, has_proj, final_norm, chunks):
    refs = list(refs)
    x_ref = refs.pop(0)
    if has_proj:
        ya_ref, yr_ref, wo_ref = refs.pop(0), refs.pop(0), refs.pop(0)
    g_ref, win_ref, wout_ref = refs.pop(0), refs.pop(0), refs.pop(0)
    if final_norm:
        gf_ref = refs.pop(0)
    o_ref = refs.pop(0)

    x = x_ref[...]
    if has_proj:
        y = jnp.concatenate([ya_ref[...], yr_ref[...]], axis=1)
        x = x + jnp.dot(y, wo_ref[...], preferred_element_type=F32)
    hn = _rmsnorm(x, g_ref[...]).astype(BF16)
    d_ff = wout_ref.shape[0]
    acc = None
    for lo, hi in zip(chunks[:-1], chunks[1:]):
        gate = jnp.dot(hn, win_ref[:, lo:hi], preferred_element_type=F32)
        up = jnp.dot(hn, win_ref[:, d_ff + lo:d_ff + hi], preferred_element_type=F32)
        act = (gate * _sigmoid(gate) * up).astype(BF16)
        part = jnp.dot(act, wout_ref[lo:hi, :], preferred_element_type=F32)
        acc = part if acc is None else acc + part
    x = x + 0.5 * acc
    if final_norm:
        x = _rmsnorm(x, gf_ref[...])
    o_ref[...] = x


def _ffn(x, g, w_in, w_out, w_idx, *, proj=None, final_g=None, tm, chunks):
    t, d = x.shape
    const = lambda i: (0, 0)
    rows = lambda i: (i, 0)

    def resident(w, lead):
        block = (None,) * len(lead) + w.shape[len(lead):]
        return pl.BlockSpec(block, lambda i: lead + (0, 0), pipeline_mode=pl.Buffered(1))

    args = [x]
    in_specs = [pl.BlockSpec((tm, d), rows)]
    if proj is not None:
        ya, yr, wo = proj
        args += [ya, yr, wo]
        in_specs += [pl.BlockSpec((tm, ya.shape[1]), rows), pl.BlockSpec((tm, yr.shape[1]), rows),
                     resident(wo, w_idx[:1])]
    args += [g, w_in, w_out]
    in_specs += [pl.BlockSpec((1, d), const), resident(w_in, w_idx), resident(w_out, w_idx)]
    if final_g is not None:
        args.append(final_g)
        in_specs.append(pl.BlockSpec((1, d), const))
    kern = functools.partial(_ffn_kernel, has_proj=proj is not None,
                             final_norm=final_g is not None, chunks=chunks)
    return pl.pallas_call(
        kern,
        grid=(t // tm,),
        in_specs=in_specs,
        out_specs=pl.BlockSpec((tm, d), rows),
        out_shape=jax.ShapeDtypeStruct((t, d), F32),
        compiler_params=pltpu.CompilerParams(
            dimension_semantics=("parallel",), vmem_limit_bytes=VMEM_LIMIT),
        name="ffn",
    )(*args)


def _pick_tile(n, target):
    t = min(n, target)
    while n % t:
        t //= 2
    return t


def _ffn_chunks(d_ff):
    if d_ff % MXU_TILE or d_ff < 2 * MXU_TILE:
        return (0, d_ff)
    tiles = d_ff // MXU_TILE
    return (0, (tiles + 1) // 2 * MXU_TILE, d_ff)


def kernel(x, norm_g, w_in, b_f, conv_w, conv_b, w_rg_a, b_rg_a, w_rg_x, b_rg_x, rg_lambda,
           w_out, w_ffn_in, w_ffn_out, final_g):
    b, s, d = x.shape
    depth = norm_g.shape[0]
    n_heads = b_f.shape[-1]
    d_rec = conv_w.shape[-1]
    n_blocks = w_rg_a.shape[1]
    d_attn = (w_in.shape[-1] - n_heads - 2 * d_rec) // 3
    hd = d_attn // n_heads
    d_ff = w_ffn_out.shape[2]
    assert 2 * hd == LANES and n_heads % 2 == 0 and n_heads <= SUBLANES
    assert d_attn % LANES == 0 and d_rec % LANES == 0 and d % LANES == 0

    tm_mix = _pick_tile(s, 512)
    tq = _pick_tile(s, 512)
    tm_ffn = _pick_tile(b * s, 1024)
    chunks = _ffn_chunks(d_ff)

    o_k, o_v, o_f = d_attn, 2 * d_attn, 3 * d_attn
    o_xr = o_f + n_heads
    tri = jnp.concatenate([jnp.triu(jnp.ones((tm_mix, tm_mix), BF16)),
                           jnp.ones((tm_mix, LANES), BF16)], axis=1)
    eye = jnp.eye(n_blocks, dtype=F32)[:, None, :, None]

    def block_diag(w):
        return (w[:, :, None, :] * eye).reshape(d_rec, d_rec).astype(BF16)

    def mixer_weights(l):
        wl = w_in[l]
        w_rows = jnp.concatenate([wl[:, :o_k], wl[:, o_v:o_f], wl[:, o_xr:]], axis=-1).astype(BF16)
        w_f = jnp.pad(wl[:, o_f:o_xr], ((0, 0), (0, BF16_ROWS - n_heads)))
        w_cols = jnp.concatenate([wl[:, o_k:o_v], w_f], axis=-1).T.astype(BF16)
        bf_rep = jnp.broadcast_to(jnp.pad(b_f[l], (0, SUBLANES - n_heads))[:, None],
                                  (SUBLANES, LANES))
        return (w_rows, w_cols, bf_rep, tri, conv_w[l], conv_b[l][None, :],
                block_diag(w_rg_a[l]), b_rg_a[l][None, :], block_diag(w_rg_x[l]),
                b_rg_x[l][None, :], rg_lambda[l][None, :])

    w_out_b = w_out.astype(BF16)
    w_ffn_in_b = w_ffn_in.astype(BF16)
    w_ffn_out_b = w_ffn_out.astype(BF16)

    xf = x.reshape(b * s, d)
    for l in range(depth):
        g = norm_g[l][:, None, :]
        xf = _ffn(xf, g[0], w_ffn_in_b, w_ffn_out_b, (l, 0), tm=tm_ffn, chunks=chunks)
        q, kt, v, ct, y_rec = _mixer_in(
            xf.reshape(b, s, d), g[1], *mixer_weights(l),
            d_attn=d_attn, d_rec=d_rec, scale=LOG2E / math.sqrt(hd), tm=tm_mix)
        c_pairs = ct[:, :n_heads, :].reshape(b, n_heads // 2, 2, s)
        y_attn = _attention(q, kt, v, c_pairs, hd=hd, tq=tq)
        last = l == depth - 1
        xf = _ffn(xf, g[2], w_ffn_in_b, w_ffn_out_b, (l, 1),
                  proj=(y_attn.reshape(b * s, d_attn), y_rec.reshape(b * s, d_rec), w_out_b),
                  final_g=final_g[None, :] if last else None, tm=tm_ffn, chunks=chunks)
    return xf.reshape(b, s, d)
```

```python
import functools
import math

import jax
import jax.numpy as jnp
from jax import lax
from jax.experimental import pallas as pl
from jax.experimental.pallas import tpu as pltpu

EPS = 1e-6
RG_C = 8.0
LANES = 128
SUBLANES = 8
BF16_ROWS = 16
MXU_TILE = 256
N_SPLIT = 3
NEG_BIG = -1e30
LOG2E = math.log2(math.e)
VMEM_LIMIT = 56 * 1024 * 1024

F32 = jnp.float32
BF16 = jnp.bfloat16
NT_DIMS = (((1,), (1,)), ((), ()))


def _rmsnorm(x, g):
    ms = jnp.mean(x * x, axis=-1, keepdims=True)
    return x * lax.rsqrt(ms + EPS) * g


def _sigmoid(x):
    return 1.0 / (1.0 + jnp.exp(-x))


def _softplus(x):
    return jnp.maximum(x, 0.0) + jnp.log1p(jnp.exp(-jnp.abs(x)))


def _gelu_tanh(x):
    c = math.sqrt(2.0 / math.pi)
    return (0.5 * x) * (1.0 + jnp.tanh(x * (c + (c * 0.044715) * (x * x))))


def _shift_rows(x, k, fill):
    rows = lax.broadcasted_iota(jnp.int32, x.shape, 0)
    return jnp.where(rows >= k, pltpu.roll(x, k, 0), fill)


def _tile_lanes(x, n):
    return jnp.concatenate([x] * n, axis=1)


def _split3(x):
    hi = x.astype(BF16).astype(F32)
    r = x - hi
    mid = r.astype(BF16).astype(F32)
    lo = (r - mid).astype(BF16).astype(F32)
    return hi, mid, lo


def _mixer_in_kernel(x_ref, g_ref, w_ref, wt_ref, bf_ref, tri_ref, cw_ref, cb_ref, wa_ref, ba_ref,
                     wx_ref, bx_ref, lam_ref,
                     q_ref, kt_ref, v_ref, ct_ref, yr_ref,
                     c_carry, h_carry, x_carry, a_s, u_s, hs_s, *, d_attn, d_rec, scale):
    j = pl.program_id(1)
    tm = x_ref.shape[1]
    nb = tm // SUBLANES

    @pl.when(j == 0)
    def _():
        c_carry[...] = jnp.zeros_like(c_carry)
        h_carry[...] = jnp.zeros_like(h_carry)
        x_carry[...] = jnp.zeros_like(x_carry)

    hn = _rmsnorm(x_ref[0], g_ref[...]).astype(BF16)
    z = jnp.dot(hn, w_ref[...], preferred_element_type=F32)
    zt = lax.dot_general(wt_ref[...], hn, NT_DIMS, preferred_element_type=F32)

    q_ref[0] = (z[:, :d_attn] * scale).astype(BF16)
    v_ref[0] = z[:, d_attn:2 * d_attn].astype(BF16)
    kt_ref[0] = zt[:d_attn].astype(BF16)
    o_xr = 2 * d_attn
    o_gr = o_xr + d_rec
    xr = z[:, o_xr:o_gr]
    gr = z[:, o_gr:]

    f_logit = zt[d_attn:d_attn + SUBLANES] + _tile_lanes(bf_ref[...], tm // LANES)
    log_f = -_softplus(-f_logit)
    parts = jnp.concatenate(_split3(log_f) + (jnp.zeros_like(log_f),), axis=0).astype(BF16)
    cs = jnp.dot(parts, tri_ref[...], preferred_element_type=F32)
    cs = cs[0:SUBLANES] + cs[SUBLANES:2 * SUBLANES] + cs[2 * SUBLANES:3 * SUBLANES]
    carry = c_carry[...]
    ct_ref[0] = cs[:, :tm] + _tile_lanes(carry, tm // LANES)
    c_carry[...] = carry + cs[:, tm:]

    width = cw_ref.shape[0]
    ext = jnp.concatenate([x_carry[...], xr], axis=0)
    x_carry[...] = xr[tm - SUBLANES:, :]
    xc = xr * cw_ref[width - 1:width, :] + cb_ref[...]
    for s in range(1, width):
        xc = xc + pltpu.roll(ext, s, 0)[SUBLANES:, :] * cw_ref[width - 1 - s:width - s, :]

    xcb = xc.astype(BF16)
    r = _sigmoid(jnp.dot(xcb, wa_ref[...], preferred_element_type=F32) + ba_ref[...])
    i = _sigmoid(jnp.dot(xcb, wx_ref[...], preferred_element_type=F32) + bx_ref[...])
    log_a = (-RG_C * _softplus(-lam_ref[...])) * r
    a = jnp.exp(log_a)
    u = jnp.sqrt(-jnp.tanh(log_a) * (1.0 + a * a)) * (i * xc)

    a3 = a.reshape(nb, SUBLANES, d_rec)
    u3 = u.reshape(nb, SUBLANES, d_rec)
    sub = lax.broadcasted_iota(jnp.int32, a3.shape, 1)
    k = 1
    while k < SUBLANES:
        keep = sub >= k
        u3 = u3 + a3 * jnp.where(keep, pltpu.roll(u3, k, 1), 0.0)
        a3 = a3 * jnp.where(keep, pltpu.roll(a3, k, 1), 1.0)
        k *= 2
    a2 = a3.reshape(tm, d_rec)
    u2 = u3.reshape(tm, d_rec)
    n_lt = d_rec // LANES
    for c in range(n_lt):
        a_s[c] = a2[:, c * LANES:(c + 1) * LANES]
        u_s[c] = u2[:, c * LANES:(c + 1) * LANES]
    last_rows = pl.ds(SUBLANES - 1, nb, stride=SUBLANES)
    ab = jnp.concatenate([a_s[c, last_rows, :] for c in range(n_lt)], axis=1)
    ub = jnp.concatenate([u_s[c, last_rows, :] for c in range(n_lt)], axis=1)
    k = 1
    while k < nb:
        ub = ub + ab * _shift_rows(ub, k, 0.0)
        ab = ab * _shift_rows(ab, k, 1.0)
        k *= 2
    h_in = h_carry[...]
    h_end = ab * h_in + ub
    h_carry[...] = h_end[nb - 1:nb, :]
    hs_s[...] = _shift_rows(h_end, 1, h_in)
    for b in range(nb):
        rows = slice(b * SUBLANES, (b + 1) * SUBLANES)
        for c in range(n_lt):
            u_s[c, rows, :] = (a_s[c, rows, :] * hs_s[b:b + 1, c * LANES:(c + 1) * LANES]
                               + u_s[c, rows, :])
    h = jnp.concatenate([u_s[c] for c in range(n_lt)], axis=1)

    yr_ref[0] = (h * _gelu_tanh(gr)).astype(BF16)


def _mixer_in(x, g, w, wt, bf_rep, tri, conv_w, conv_b, wa, ba, wx, bx, lam, *, d_attn, d_rec, scale, tm):
    b, s, d = x.shape
    const = lambda *_: (0, 0)
    row3 = lambda bi, j: (bi, j, 0)
    col3 = lambda bi, j: (bi, 0, j)
    kern = functools.partial(_mixer_in_kernel, d_attn=d_attn, d_rec=d_rec, scale=scale)
    return pl.pallas_call(
        kern,
        grid=(b, s // tm),
        in_specs=[
            pl.BlockSpec((1, tm, d), row3),
            pl.BlockSpec((1, d), const),
            pl.BlockSpec(w.shape, const),
            pl.BlockSpec(wt.shape, const),
            pl.BlockSpec((SUBLANES, LANES), const),
            pl.BlockSpec(tri.shape, const),
            pl.BlockSpec(conv_w.shape, const),
            pl.BlockSpec((1, d_rec), const),
            pl.BlockSpec((d_rec, d_rec), const),
            pl.BlockSpec((1, d_rec), const),
            pl.BlockSpec((d_rec, d_rec), const),
            pl.BlockSpec((1, d_rec), const),
            pl.BlockSpec((1, d_rec), const),
        ],
        out_specs=[
            pl.BlockSpec((1, tm, d_attn), row3),
            pl.BlockSpec((1, d_attn, tm), col3),
            pl.BlockSpec((1, tm, d_attn), row3),
            pl.BlockSpec((1, SUBLANES, tm), col3),
            pl.BlockSpec((1, tm, d_rec), row3),
        ],
        out_shape=[
            jax.ShapeDtypeStruct((b, s, d_attn), BF16),
            jax.ShapeDtypeStruct((b, d_attn, s), BF16),
            jax.ShapeDtypeStruct((b, s, d_attn), BF16),
            jax.ShapeDtypeStruct((b, SUBLANES, s), F32),
            jax.ShapeDtypeStruct((b, s, d_rec), BF16),
        ],
        scratch_shapes=[
            pltpu.VMEM((SUBLANES, LANES), F32),
            pltpu.VMEM((1, d_rec), F32),
            pltpu.VMEM((SUBLANES, d_rec), F32),
            pltpu.VMEM((d_rec // LANES, tm, LANES), F32),
            pltpu.VMEM((d_rec // LANES, tm, LANES), F32),
            pltpu.VMEM((tm // SUBLANES, d_rec), F32),
        ],
        compiler_params=pltpu.CompilerParams(
            dimension_semantics=("parallel", "arbitrary"), vmem_limit_bytes=VMEM_LIMIT),
        name="mixer_in",
    )(x, g, w, wt, bf_rep, tri, conv_w, conv_b, wa, ba, wx, bx, lam)


def _attn_kernel(q_ref, kt_ref, v_ref, c_ref, o_ref, ka_ref, kb_ref, vbd_ref, acc_ref, m_ref,
                 se_ref, so_ref, *, tq, hd):
    s_len = q_ref.shape[1]
    n_hp = q_ref.shape[2] // LANES
    nq = s_len // tq
    half = tq // 2
    lane = lax.broadcasted_iota(jnp.int32, (1, LANES), 1)
    is_a = lane < hd
    sub = lax.broadcasted_iota(jnp.int32, (BF16_ROWS, s_len), 0)
    ones_a = jnp.where((lane >= hd) & (lane < hd + N_SPLIT), 1.0, 0.0).astype(BF16)
    ones_b = jnp.where(lane < N_SPLIT, 1.0, 0.0).astype(BF16)

    def setup(p):
        l0 = p * LANES

        def bias_rows(h):
            hi, mid, lo = _split3(c_ref[0, p, h:h + 1, :] * (-LOG2E))
            rows = jnp.where(sub == 0, hi, jnp.where(sub == 1, mid, jnp.where(sub == 2, lo, 0.0)))
            return rows.astype(BF16)

        zeros_k = jnp.zeros((hd - BF16_ROWS, s_len), BF16)
        ka_ref[p, 0:hd, :] = kt_ref[0, l0:l0 + hd, :]
        ka_ref[p, hd:hd + BF16_ROWS, :] = bias_rows(0)
        ka_ref[p, hd + BF16_ROWS:, :] = zeros_k
        kb_ref[p, 0:BF16_ROWS, :] = bias_rows(1)
        kb_ref[p, BF16_ROWS:hd, :] = zeros_k
        kb_ref[p, hd:, :] = kt_ref[0, l0 + hd:l0 + LANES, :]

        zeros_v = jnp.zeros((half, LANES), BF16)
        ones_v = jnp.ones((half, LANES), BF16)
        for jb in range(s_len // half):
            v2 = v_ref[0, jb * half:(jb + 1) * half, l0:l0 + LANES]
            vbd_ref[p, jb, 0:half, 0:LANES] = jnp.where(is_a, v2, ones_v)
            vbd_ref[p, jb, 0:half, LANES:] = zeros_v
            vbd_ref[p, jb, half:, 0:LANES] = zeros_v
            vbd_ref[p, jb, half:, LANES:] = jnp.where(is_a, ones_v, v2)

    row_t = lax.broadcasted_iota(jnp.int32, (half, half), 0)
    col_t = lax.broadcasted_iota(jnp.int32, (half, half), 1)
    mask_top = col_t <= row_t
    row_b = lax.broadcasted_iota(jnp.int32, (half, tq), 0)
    col_b = lax.broadcasted_iota(jnp.int32, (half, tq), 1)
    mask_bot = col_b <= row_b + half

    def q_tile(p, qi, first_buf, parity):
        q0, l0 = qi * tq, p * LANES
        k_aug = (ka_ref.at[p], kb_ref.at[p])
        vbd = vbd_ref.at[p]
        q2 = q_ref[0, q0:q0 + tq, l0:l0 + LANES]
        q_heads = (jnp.where(is_a, q2, ones_a), jnp.where(is_a, ones_b, q2))
        bufs = (se_ref, so_ref) if first_buf == 0 else (so_ref, se_ref)
        acc = acc_ref.at[parity]
        m_run = m_ref.at[parity]
        m_run[...] = jnp.full(m_run.shape, NEG_BIG, F32)
        acc[...] = jnp.zeros(acc.shape, F32)

        def qk(j, dst):
            for h in range(2):
                dst[h] = jnp.dot(q_heads[h], k_aug[h][:, j * tq:(j + 1) * tq],
                                 preferred_element_type=F32)

        def softmax_pv(s_heads, r0, jb0, mask):
            nr, tk = s_heads[0].shape
            nblk = tk // half
            ps, alphas = [], []
            for h in range(2):
                s = s_heads[h]
                if mask is not None:
                    s = jnp.where(mask, s, NEG_BIG)
                m_old = m_run[h, r0:r0 + nr]
                m_new = jnp.maximum(m_old, jnp.max(s, axis=1, keepdims=True))
                alphas.append(jnp.exp2(m_old - m_new))
                ps.append(jnp.exp2((s - _tile_lanes(m_new, tk // LANES)).astype(BF16)))
                m_run[h, r0:r0 + nr] = m_new
            p_cat = jnp.concatenate([ps[h][:, i * half:(i + 1) * half]
                                     for i in range(nblk) for h in range(2)], axis=1)
            v_blk = vbd[jb0:jb0 + nblk].reshape(nblk * 2 * half, 2 * LANES)
            pv = jnp.dot(p_cat, v_blk, preferred_element_type=F32)
            acc[r0:r0 + nr] = jnp.concatenate(alphas, axis=1) * acc[r0:r0 + nr] + pv

        qk(0, bufs[0])
        for j in range(qi):
            src, dst = bufs[j % 2], bufs[(j + 1) % 2]
            s_heads = (src[0], src[1])
            qk(j + 1, dst)
            softmax_pv(s_heads, 0, 2 * j, None)
        src = bufs[qi % 2]
        softmax_pv((src[0, 0:half, 0:half], src[1, 0:half, 0:half]), 0, 2 * qi, mask_top)
        softmax_pv((src[0, half:, :], src[1, half:, :]), half, 2 * qi, mask_bot)

        acc_a = acc[:, :LANES]
        acc_b = acc[:, LANES:]
        num = jnp.where(is_a, acc_a, acc_b)
        den = pltpu.roll(jnp.where(is_a, acc_b, acc_a), hd, 1)
        o_ref[0, q0:q0 + tq, l0:l0 + LANES] = (num / den).astype(BF16)

    setup(0)
    first_buf, count = 0, 0
    for p in range(n_hp):
        for qi in range(nq):
            q_tile(p, qi, first_buf, count % 2)
            first_buf = 1 - (first_buf + qi) % 2
            count += 1
            if qi == nq // 2 and p + 1 < n_hp:
                setup(p + 1)


def _attention(q, kt, v, c_pairs, *, hd, tq, n_hp):
    b, s, d_attn = q.shape
    n_pairs = d_attn // LANES
    blk = lambda bi, p: (bi, 0, p)
    kern = functools.partial(_attn_kernel, tq=tq, hd=hd)
    return pl.pallas_call(
        kern,
        grid=(b, n_pairs // n_hp),
        in_specs=[
            pl.BlockSpec((1, s, n_hp * LANES), blk),
            pl.BlockSpec((1, n_hp * LANES, s), lambda bi, p: (bi, p, 0)),
            pl.BlockSpec((1, s, n_hp * LANES), blk),
            pl.BlockSpec((1, n_hp, 2, s), lambda bi, p: (bi, p, 0, 0)),
        ],
        out_specs=pl.BlockSpec((1, s, n_hp * LANES), blk),
        out_shape=jax.ShapeDtypeStruct((b, s, d_attn), BF16),
        scratch_shapes=[
            pltpu.VMEM((n_hp, LANES, s), BF16),
            pltpu.VMEM((n_hp, LANES, s), BF16),
            pltpu.VMEM((n_hp, 2 * s // tq, tq, 2 * LANES), BF16),
            pltpu.VMEM((2, tq, 2 * LANES), F32),
            pltpu.VMEM((2, 2, tq, LANES), F32),
            pltpu.VMEM((2, tq, tq), F32),
            pltpu.VMEM((2, tq, tq), F32),
        ],
        compiler_params=pltpu.CompilerParams(
            dimension_semantics=("parallel", "parallel"), vmem_limit_bytes=VMEM_LIMIT),
        name="fox_attn",
    )(q, kt, v, c_pairs)


def _ffn_kernel(*refs, has_proj, final_norm, chunks):
    refs = list(refs)
    x_ref = refs.pop(0)
    if has_proj:
        ya_ref, yr_ref, wo_ref = refs.pop(0), refs.pop(0), refs.pop(0)
    g_ref, win_ref, wout_ref = refs.pop(0), refs.pop(0), refs.pop(0)
    if final_norm:
        gf_ref = refs.pop(0)
    o_ref = refs.pop(0)

    x = x_ref[...]
    if has_proj:
        y = jnp.concatenate([ya_ref[...], yr_ref[...]], axis=1)
        x = x + jnp.dot(y, wo_ref[...], preferred_element_type=F32)
    hn = _rmsnorm(x, g_ref[...]).astype(BF16)
    d_ff = wout_ref.shape[0]
    acc = None
    for lo, hi in zip(chunks[:-1], chunks[1:]):
        gate = jnp.dot(hn, win_ref[:, lo:hi], preferred_element_type=F32)
        up = jnp.dot(hn, win_ref[:, d_ff + lo:d_ff + hi], preferred_element_type=F32)
        act = (gate * _sigmoid(gate) * up).astype(BF16)
        part = jnp.dot(act, wout_ref[lo:hi, :], preferred_element_type=F32)
        acc = part if acc is None else acc + part
    x = x + 0.5 * acc
    if final_norm:
        x = _rmsnorm(x, gf_ref[...])
    o_ref[...] = x


def _ffn(x, g, w_in, w_out, w_idx, *, proj=None, final_g=None, tm, chunks):
    t, d = x.shape
    const = lambda i: (0, 0)
    rows = lambda i: (i, 0)

    def resident(w, lead):
        block = (None,) * len(lead) + w.shape[len(lead):]
        return pl.BlockSpec(block, lambda i: lead + (0, 0), pipeline_mode=pl.Buffered(1))

    args = [x]
    in_specs = [pl.BlockSpec((tm, d), rows)]
    if proj is not None:
        ya, yr, wo = proj
        args += [ya, yr, wo]
        in_specs += [pl.BlockSpec((tm, ya.shape[1]), rows), pl.BlockSpec((tm, yr.shape[1]), rows),
                     resident(wo, w_idx[:1])]
    args += [g, w_in, w_out]
    in_specs += [pl.BlockSpec((1, d), const), resident(w_in, w_idx), resident(w_out, w_idx)]
    if final_g is not None:
        args.append(final_g)
        in_specs.append(pl.BlockSpec((1, d), const))
    kern = functools.partial(_ffn_kernel, has_proj=proj is not None,
                             final_norm=final_g is not None, chunks=chunks)
    return pl.pallas_call(
        kern,
        grid=(t // tm,),
        in_specs=in_specs,
        out_specs=pl.BlockSpec((tm, d), rows),
        out_shape=jax.ShapeDtypeStruct((t, d), F32),
        compiler_params=pltpu.CompilerParams(
            dimension_semantics=("parallel",), vmem_limit_bytes=VMEM_LIMIT),
        name="ffn",
    )(*args)


def _pick_tile(n, target):
    t = min(n, target)
    while n % t:
        t //= 2
    return t


def _ffn_chunks(d_ff):
    if d_ff % MXU_TILE or d_ff < 2 * MXU_TILE:
        return (0, d_ff)
    tiles = d_ff // MXU_TILE
    return (0, (tiles + 1) // 2 * MXU_TILE, d_ff)


def kernel(x, norm_g, w_in, b_f, conv_w, conv_b, w_rg_a, b_rg_a, w_rg_x, b_rg_x, rg_lambda,
           w_out, w_ffn_in, w_ffn_out, final_g):
    b, s, d = x.shape
    depth = norm_g.shape[0]
    n_heads = b_f.shape[-1]
    d_rec = conv_w.shape[-1]
    n_blocks = w_rg_a.shape[1]
    d_attn = (w_in.shape[-1] - n_heads - 2 * d_rec) // 3
    hd = d_attn // n_heads
    d_ff = w_ffn_out.shape[2]
    assert 2 * hd == LANES and n_heads % 2 == 0 and n_heads <= SUBLANES
    assert d_attn % LANES == 0 and d_rec % LANES == 0 and d % LANES == 0

    tm_mix = _pick_tile(s, 512)
    tq = _pick_tile(s, 512)
    tm_ffn = _pick_tile(b * s, 1024)
    chunks = _ffn_chunks(d_ff)
    n_hp = 2 if (n_heads // 2) % 2 == 0 else 1

    o_k, o_v, o_f = d_attn, 2 * d_attn, 3 * d_attn
    o_xr = o_f + n_heads
    tri = jnp.concatenate([jnp.triu(jnp.ones((tm_mix, tm_mix), BF16)),
                           jnp.ones((tm_mix, LANES), BF16)], axis=1)
    eye = jnp.eye(n_blocks, dtype=F32)[:, None, :, None]

    def block_diag(w):
        return (w[:, :, None, :] * eye).reshape(d_rec, d_rec).astype(BF16)

    def mixer_weights(l):
        wl = w_in[l]
        w_rows = jnp.concatenate([wl[:, :o_k], wl[:, o_v:o_f], wl[:, o_xr:]], axis=-1).astype(BF16)
        w_f = jnp.pad(wl[:, o_f:o_xr], ((0, 0), (0, BF16_ROWS - n_heads)))
        w_cols = jnp.concatenate([wl[:, o_k:o_v], w_f], axis=-1).T.astype(BF16)
        bf_rep = jnp.broadcast_to(jnp.pad(b_f[l], (0, SUBLANES - n_heads))[:, None],
                                  (SUBLANES, LANES))
        return (w_rows, w_cols, bf_rep, tri, conv_w[l], conv_b[l][None, :],
                block_diag(w_rg_a[l]), b_rg_a[l][None, :], block_diag(w_rg_x[l]),
                b_rg_x[l][None, :], rg_lambda[l][None, :])

    w_out_b = w_out.astype(BF16)
    w_ffn_in_b = w_ffn_in.astype(BF16)
    w_ffn_out_b = w_ffn_out.astype(BF16)

    xf = x.reshape(b * s, d)
    for l in range(depth):
        g = norm_g[l][:, None, :]
        xf = _ffn(xf, g[0], w_ffn_in_b, w_ffn_out_b, (l, 0), tm=tm_ffn, chunks=chunks)
        q, kt, v, ct, y_rec = _mixer_in(
            xf.reshape(b, s, d), g[1], *mixer_weights(l),
            d_attn=d_attn, d_rec=d_rec, scale=LOG2E / math.sqrt(hd), tm=tm_mix)
        c_pairs = ct[:, :n_heads, :].reshape(b, n_heads // 2, 2, s)
        y_attn = _attention(q, kt, v, c_pairs, hd=hd, tq=tq, n_hp=n_hp)
        last = l == depth - 1
        xf = _ffn(xf, g[2], w_ffn_in_b, w_ffn_out_b, (l, 1),
                  proj=(y_attn.reshape(b * s, d_attn), y_rec.reshape(b * s, d_rec), w_out_b),
                  final_g=final_g[None, :] if last else None, tm=tm_ffn, chunks=chunks)
    return xf.reshape(b, s, d)
```

```python
import functools
import math

import jax
import jax.numpy as jnp
from jax import lax
from jax.experimental import pallas as pl
from jax.experimental.pallas import tpu as pltpu

EPS = 1e-6
RG_C = 8.0
LANES = 128
SUBLANES = 8
BF16_ROWS = 16
MXU_TILE = 256
N_SPLIT = 3
NEG_BIG = -1e30
LOG2E = math.log2(math.e)
VMEM_LIMIT = 56 * 1024 * 1024

F32 = jnp.float32
BF16 = jnp.bfloat16
NT_DIMS = (((1,), (1,)), ((), ()))


def _rmsnorm(x, g):
    ms = jnp.mean(x * x, axis=-1, keepdims=True)
    return x * lax.rsqrt(ms + EPS) * g


def _sigmoid(x):
    return 1.0 / (1.0 + jnp.exp(-x))


def _softplus(x):
    return jnp.maximum(x, 0.0) + jnp.log1p(jnp.exp(-jnp.abs(x)))


def _gelu_tanh(x):
    c = math.sqrt(2.0 / math.pi)
    return (0.5 * x) * (1.0 + jnp.tanh(x * (c + (c * 0.044715) * (x * x))))


def _shift_rows(x, k, fill):
    rows = lax.broadcasted_iota(jnp.int32, x.shape, 0)
    return jnp.where(rows >= k, pltpu.roll(x, k, 0), fill)


def _tile_lanes(x, n):
    return jnp.concatenate([x] * n, axis=1)


def _split3(x):
    hi = x.astype(BF16).astype(F32)
    r = x - hi
    mid = r.astype(BF16).astype(F32)
    lo = (r - mid).astype(BF16).astype(F32)
    return hi, mid, lo


def _mixer_in_kernel(x_ref, g_ref, w_ref, wt_ref, bf_ref, tri_ref, cw_ref, cb_ref, wa_ref, ba_ref,
                     wx_ref, bx_ref, lam_ref,
                     q_ref, kt_ref, v_ref, ct_ref, yr_ref,
                     c_carry, h_carry, x_carry, a_s, u_s, hs_s, *, d_attn, d_rec, scale):
    j = pl.program_id(1)
    tm = x_ref.shape[1]
    nb = tm // SUBLANES

    @pl.when(j == 0)
    def _():
        c_carry[...] = jnp.zeros_like(c_carry)
        h_carry[...] = jnp.zeros_like(h_carry)
        x_carry[...] = jnp.zeros_like(x_carry)

    hn = _rmsnorm(x_ref[0], g_ref[...]).astype(BF16)
    z = jnp.dot(hn, w_ref[...], preferred_element_type=F32)
    zt = lax.dot_general(wt_ref[...], hn, NT_DIMS, preferred_element_type=F32)

    q_ref[0] = (z[:, :d_attn] * scale).astype(BF16)
    v_ref[0] = z[:, d_attn:2 * d_attn].astype(BF16)
    kt_ref[0] = zt[:d_attn].astype(BF16)
    o_xr = 2 * d_attn
    o_gr = o_xr + d_rec
    xr = z[:, o_xr:o_gr]
    gr = z[:, o_gr:]

    f_logit = zt[d_attn:d_attn + SUBLANES] + _tile_lanes(bf_ref[...], tm // LANES)
    log_f = -_softplus(-f_logit)
    parts = jnp.concatenate(_split3(log_f) + (jnp.zeros_like(log_f),), axis=0).astype(BF16)
    cs = jnp.dot(parts, tri_ref[...], preferred_element_type=F32)
    cs = cs[0:SUBLANES] + cs[SUBLANES:2 * SUBLANES] + cs[2 * SUBLANES:3 * SUBLANES]
    carry = c_carry[...]
    ct_ref[0] = cs[:, :tm] + _tile_lanes(carry, tm // LANES)
    c_carry[...] = carry + cs[:, tm:]

    width = cw_ref.shape[0]
    ext = jnp.concatenate([x_carry[...], xr], axis=0)
    x_carry[...] = xr[tm - SUBLANES:, :]
    xc = xr * cw_ref[width - 1:width, :] + cb_ref[...]
    for s in range(1, width):
        xc = xc + pltpu.roll(ext, s, 0)[SUBLANES:, :] * cw_ref[width - 1 - s:width - s, :]

    xcb = xc.astype(BF16)
    r = _sigmoid(jnp.dot(xcb, wa_ref[...], preferred_element_type=F32) + ba_ref[...])
    i = _sigmoid(jnp.dot(xcb, wx_ref[...], preferred_element_type=F32) + bx_ref[...])
    log_a = (-RG_C * _softplus(-lam_ref[...])) * r
    a = jnp.exp(log_a)
    u = jnp.sqrt(-jnp.tanh(log_a) * (1.0 + a * a)) * (i * xc)

    a3 = a.reshape(nb, SUBLANES, d_rec)
    u3 = u.reshape(nb, SUBLANES, d_rec)
    sub = lax.broadcasted_iota(jnp.int32, a3.shape, 1)
    k = 1
    while k < SUBLANES:
        keep = sub >= k
        u3 = u3 + a3 * jnp.where(keep, pltpu.roll(u3, k, 1), 0.0)
        a3 = a3 * jnp.where(keep, pltpu.roll(a3, k, 1), 1.0)
        k *= 2
    a2 = a3.reshape(tm, d_rec)
    u2 = u3.reshape(tm, d_rec)
    n_lt = d_rec // LANES
    for c in range(n_lt):
        a_s[c] = a2[:, c * LANES:(c + 1) * LANES]
        u_s[c] = u2[:, c * LANES:(c + 1) * LANES]
    last_rows = pl.ds(SUBLANES - 1, nb, stride=SUBLANES)
    ab = jnp.concatenate([a_s[c, last_rows, :] for c in range(n_lt)], axis=1)
    ub = jnp.concatenate([u_s[c, last_rows, :] for c in range(n_lt)], axis=1)
    k = 1
    while k < nb:
        ub = ub + ab * _shift_rows(ub, k, 0.0)
        ab = ab * _shift_rows(ab, k, 1.0)
        k *= 2
    h_in = h_carry[...]
    h_end = ab * h_in + ub
    h_carry[...] = h_end[nb - 1:nb, :]
    hs_s[...] = _shift_rows(h_end, 1, h_in)
    for b in range(nb):
        rows = slice(b * SUBLANES, (b + 1) * SUBLANES)
        for c in range(n_lt):
            u_s[c, rows, :] = (a_s[c, rows, :] * hs_s[b:b + 1, c * LANES:(c + 1) * LANES]
                               + u_s[c, rows, :])
    h = jnp.concatenate([u_s[c] for c in range(n_lt)], axis=1)

    yr_ref[0] = (h * _gelu_tanh(gr)).astype(BF16)


def _mixer_in(x, g, w, wt, bf_rep, tri, conv_w, conv_b, wa, ba, wx, bx, lam, *, d_attn, d_rec, scale, tm):
    b, s, d = x.shape
    const = lambda *_: (0, 0)
    row3 = lambda bi, j: (bi, j, 0)
    col3 = lambda bi, j: (bi, 0, j)
    kern = functools.partial(_mixer_in_kernel, d_attn=d_attn, d_rec=d_rec, scale=scale)
    return pl.pallas_call(
        kern,
        grid=(b, s // tm),
        in_specs=[
            pl.BlockSpec((1, tm, d), row3),
            pl.BlockSpec((1, d), const),
            pl.BlockSpec(w.shape, const),
            pl.BlockSpec(wt.shape, const),
            pl.BlockSpec((SUBLANES, LANES), const),
            pl.BlockSpec(tri.shape, const),
            pl.BlockSpec(conv_w.shape, const),
            pl.BlockSpec((1, d_rec), const),
            pl.BlockSpec((d_rec, d_rec), const),
            pl.BlockSpec((1, d_rec), const),
            pl.BlockSpec((d_rec, d_rec), const),
            pl.BlockSpec((1, d_rec), const),
            pl.BlockSpec((1, d_rec), const),
        ],
        out_specs=[
            pl.BlockSpec((1, tm, d_attn), row3),
            pl.BlockSpec((1, d_attn, tm), col3),
            pl.BlockSpec((1, tm, d_attn), row3),
            pl.BlockSpec((1, SUBLANES, tm), col3),
            pl.BlockSpec((1, tm, d_rec), row3),
        ],
        out_shape=[
            jax.ShapeDtypeStruct((b, s, d_attn), BF16),
            jax.ShapeDtypeStruct((b, d_attn, s), BF16),
            jax.ShapeDtypeStruct((b, s, d_attn), BF16),
            jax.ShapeDtypeStruct((b, SUBLANES, s), F32),
            jax.ShapeDtypeStruct((b, s, d_rec), BF16),
        ],
        scratch_shapes=[
            pltpu.VMEM((SUBLANES, LANES), F32),
            pltpu.VMEM((1, d_rec), F32),
            pltpu.VMEM((SUBLANES, d_rec), F32),
            pltpu.VMEM((d_rec // LANES, tm, LANES), F32),
            pltpu.VMEM((d_rec // LANES, tm, LANES), F32),
            pltpu.VMEM((tm // SUBLANES, d_rec), F32),
        ],
        compiler_params=pltpu.CompilerParams(
            dimension_semantics=("parallel", "arbitrary"), vmem_limit_bytes=VMEM_LIMIT),
        name="mixer_in",
    )(x, g, w, wt, bf_rep, tri, conv_w, conv_b, wa, ba, wx, bx, lam)


def _attn_kernel(q_ref, kt_ref, v_ref, c_ref, o_ref, ka_ref, kb_ref, vbd_ref, acc_ref, m_ref,
                 se_ref, so_ref, *, tq, hd):
    s_len = q_ref.shape[1]
    nq = s_len // tq
    half = tq // 2
    lane = lax.broadcasted_iota(jnp.int32, (1, LANES), 1)
    is_a = lane < hd

    sub = lax.broadcasted_iota(jnp.int32, (BF16_ROWS, s_len), 0)

    def bias_rows(h):
        hi, mid, lo = _split3(c_ref[0, 0, h:h + 1, :] * (-LOG2E))
        rows = jnp.where(sub == 0, hi, jnp.where(sub == 1, mid, jnp.where(sub == 2, lo, 0.0)))
        return rows.astype(BF16)

    zeros_k = jnp.zeros((hd - BF16_ROWS, s_len), BF16)
    ka_ref[0:hd, :] = kt_ref[0, 0:hd, :]
    ka_ref[hd:hd + BF16_ROWS, :] = bias_rows(0)
    ka_ref[hd + BF16_ROWS:, :] = zeros_k
    kb_ref[0:BF16_ROWS, :] = bias_rows(1)
    kb_ref[BF16_ROWS:hd, :] = zeros_k
    kb_ref[hd:, :] = kt_ref[0, hd:, :]
    k_aug = (ka_ref, kb_ref)
    ones_a = jnp.where((lane >= hd) & (lane < hd + N_SPLIT), 1.0, 0.0).astype(BF16)
    ones_b = jnp.where(lane < N_SPLIT, 1.0, 0.0).astype(BF16)

    zeros_v = jnp.zeros((half, LANES), BF16)
    ones_v = jnp.ones((half, LANES), BF16)
    for jb in range(s_len // half):
        v2 = v_ref[0, jb * half:(jb + 1) * half, :]
        vbd_ref[jb, 0:half, 0:LANES] = jnp.where(is_a, v2, ones_v)
        vbd_ref[jb, 0:half, LANES:] = zeros_v
        vbd_ref[jb, half:, 0:LANES] = zeros_v
        vbd_ref[jb, half:, LANES:] = jnp.where(is_a, ones_v, v2)

    row_t = lax.broadcasted_iota(jnp.int32, (half, half), 0)
    col_t = lax.broadcasted_iota(jnp.int32, (half, half), 1)
    mask_top = col_t <= row_t
    row_b = lax.broadcasted_iota(jnp.int32, (half, tq), 0)
    col_b = lax.broadcasted_iota(jnp.int32, (half, tq), 1)
    mask_bot = col_b <= row_b + half

    def q_tile(qi, first_buf):
        q0 = qi * tq
        q2 = q_ref[0, q0:q0 + tq, :]
        q_heads = (jnp.where(is_a, q2, ones_a), jnp.where(is_a, ones_b, q2))
        bufs = (se_ref, so_ref) if first_buf == 0 else (so_ref, se_ref)
        acc = acc_ref.at[qi % 2]
        m_run = m_ref.at[qi % 2]
        m_run[...] = jnp.full(m_run.shape, NEG_BIG, F32)
        acc[...] = jnp.zeros(acc.shape, F32)

        def qk(j, dst):
            for h in range(2):
                dst[h] = jnp.dot(q_heads[h], k_aug[h][:, j * tq:(j + 1) * tq],
                                 preferred_element_type=F32)

        def softmax_pv(s_heads, r0, jb0, mask):
            nr, tk = s_heads[0].shape
            nblk = tk // half
            ps, alphas = [], []
            for h in range(2):
                s = s_heads[h]
                if mask is not None:
                    s = jnp.where(mask, s, NEG_BIG)
                m_old = m_run[h, r0:r0 + nr]
                m_new = jnp.maximum(m_old, jnp.max(s, axis=1, keepdims=True))
                alphas.append(jnp.exp2(m_old - m_new))
                ps.append(jnp.exp2((s - _tile_lanes(m_new, tk // LANES)).astype(BF16)))
                m_run[h, r0:r0 + nr] = m_new
            p_cat = jnp.concatenate([ps[h][:, i * half:(i + 1) * half]
                                     for i in range(nblk) for h in range(2)], axis=1)
            v_blk = vbd_ref[jb0:jb0 + nblk].reshape(nblk * 2 * half, 2 * LANES)
            pv = jnp.dot(p_cat, v_blk, preferred_element_type=F32)
            acc[r0:r0 + nr] = jnp.concatenate(alphas, axis=1) * acc[r0:r0 + nr] + pv

        qk(0, bufs[0])
        for j in range(qi):
            src, dst = bufs[j % 2], bufs[(j + 1) % 2]
            s_heads = (src[0], src[1])
            qk(j + 1, dst)
            softmax_pv(s_heads, 0, 2 * j, None)
        src = bufs[qi % 2]
        softmax_pv((src[0, 0:half, 0:half], src[1, 0:half, 0:half]), 0, 2 * qi, mask_top)
        softmax_pv((src[0, half:, :], src[1, half:, :]), half, 2 * qi, mask_bot)

        acc_a = acc[:, :LANES]
        acc_b = acc[:, LANES:]
        num = jnp.where(is_a, acc_a, acc_b)
        den = pltpu.roll(jnp.where(is_a, acc_b, acc_a), hd, 1)
        o_ref[0, q0:q0 + tq, :] = (num / den).astype(BF16)

    first_buf = 0
    for qi in range(nq):
        q_tile(qi, first_buf)
        first_buf = 1 - (first_buf + qi) % 2


def _attention(q, kt, v, c_pairs, *, hd, tq):
    b, s, d_attn = q.shape
    n_pairs = d_attn // LANES
    blk = lambda bi, p: (bi, 0, p)
    kern = functools.partial(_attn_kernel, tq=tq, hd=hd)
    return pl.pallas_call(
        kern,
        grid=(b, n_pairs),
        in_specs=[
            pl.BlockSpec((1, s, LANES), blk),
            pl.BlockSpec((1, LANES, s), lambda bi, p: (bi, p, 0)),
            pl.BlockSpec((1, s, LANES), blk),
            pl.BlockSpec((1, 1, 2, s), lambda bi, p: (bi, p, 0, 0)),
        ],
        out_specs=pl.BlockSpec((1, s, LANES), blk),
        out_shape=jax.ShapeDtypeStruct((b, s, d_attn), BF16),
        scratch_shapes=[
            pltpu.VMEM((LANES, s), BF16),
            pltpu.VMEM((LANES, s), BF16),
            pltpu.VMEM((2 * s // tq, tq, 2 * LANES), BF16),
            pltpu.VMEM((2, tq, 2 * LANES), F32),
            pltpu.VMEM((2, 2, tq, LANES), F32),
            pltpu.VMEM((2, tq, tq), F32),
            pltpu.VMEM((2, tq, tq), F32),
        ],
        compiler_params=pltpu.CompilerParams(
            dimension_semantics=("parallel", "parallel"), vmem_limit_bytes=VMEM_LIMIT),
        name="fox_attn",
    )(q, kt, v, c_pairs)


def _ffn_kernel(*refs, has_proj, final_norm, chunks):
    refs = list(refs)
    x_ref = refs.pop(0)
    if has_proj:
        ya_ref, yr_ref, wo_ref = refs.pop(0), refs.pop(0), refs.pop(0)
    g_ref, win_ref, wout_ref = refs.pop(0), refs.pop(0), refs.pop(0)
    if final_norm:
        gf_ref = refs.pop(0)
    o_ref = refs.pop(0)

    x = x_ref[...]
    if has_proj:
        y = jnp.concatenate([ya_ref[...], yr_ref[...]], axis=1)
        x = x + jnp.dot(y, wo_ref[...], preferred_element_type=F32)
    hn = _rmsnorm(x, g_ref[...]).astype(BF16)
    d_ff = wout_ref.shape[0]
    acc = None
    for lo, hi in zip(chunks[:-1], chunks[1:]):
        gate = jnp.dot(hn, win_ref[:, lo:hi], preferred_element_type=F32)
        up = jnp.dot(hn, win_ref[:, d_ff + lo:d_ff + hi], preferred_element_type=F32)
        act = (gate * _sigmoid(gate) * up).astype(BF16)
        part = jnp.dot(act, wout_ref[lo:hi, :], preferred_element_type=F32)
        acc = part if acc is None else acc + part
    x = x + 0.5 * acc
    if final_norm:
        x = _rmsnorm(x, gf_ref[...])
    o_ref[...] = x


def _ffn(x, g, w_in, w_out, w_idx, *, proj=None, final_g=None, tm, chunks):
    t, d = x.shape
    const = lambda i: (0, 0)
    rows = lambda i: (i, 0)

    def resident(w, lead):
        block = (None,) * len(lead) + w.shape[len(lead):]
        return pl.BlockSpec(block, lambda i: lead + (0, 0), pipeline_mode=pl.Buffered(1))

    args = [x]
    in_specs = [pl.BlockSpec((tm, d), rows)]
    if proj is not None:
        ya, yr, wo = proj
        args += [ya, yr, wo]
        in_specs += [pl.BlockSpec((tm, ya.shape[1]), rows), pl.BlockSpec((tm, yr.shape[1]), rows),
                     resident(wo, w_idx[:1])]
    args += [g, w_in, w_out]
    in_specs += [pl.BlockSpec((1, d), const), resident(w_in, w_idx), resident(w_out, w_idx)]
    if final_g is not None:
        args.append(final_g)
        in_specs.append(pl.BlockSpec((1, d), const))
    kern = functools.partial(_ffn_kernel, has_proj=proj is not None,
                             final_norm=final_g is not None, chunks=chunks)
    return pl.pallas_call(
        kern,
        grid=(t // tm,),
        in_specs=in_specs,
        out_specs=pl.BlockSpec((tm, d), rows),
        out_shape=jax.ShapeDtypeStruct((t, d), F32),
        compiler_params=pltpu.CompilerParams(
            dimension_semantics=("parallel",), vmem_limit_bytes=VMEM_LIMIT),
        name="ffn",
    )(*args)


def _pick_tile(n, target):
    t = min(n, target)
    while n % t:
        t //= 2
    return t


def _ffn_chunks(d_ff):
    if d_ff % MXU_TILE or d_ff < 2 * MXU_TILE:
        return (0, d_ff)
    tiles = d_ff // MXU_TILE
    return (0, (tiles + 1) // 2 * MXU_TILE, d_ff)


def kernel(x, norm_g, w_in, b_f, conv_w, conv_b, w_rg_a, b_rg_a, w_rg_x, b_rg_x, rg_lambda,
           w_out, w_ffn_in, w_ffn_out, final_g):
    b, s, d = x.shape
    depth = norm_g.shape[0]
    n_heads = b_f.shape[-1]
    d_rec = conv_w.shape[-1]
    n_blocks = w_rg_a.shape[1]
    d_attn = (w_in.shape[-1] - n_heads - 2 * d_rec) // 3
    hd = d_attn // n_heads
    d_ff = w_ffn_out.shape[2]
    assert 2 * hd == LANES and n_heads % 2 == 0 and n_heads <= SUBLANES
    assert d_attn % LANES == 0 and d_rec % LANES == 0 and d % LANES == 0

    tm_mix = _pick_tile(s, 1024)
    tq = _pick_tile(s, 512)
    tm_ffn = _pick_tile(b * s, 1024)
    chunks = _ffn_chunks(d_ff)

    o_k, o_v, o_f = d_attn, 2 * d_attn, 3 * d_attn
    o_xr = o_f + n_heads
    tri = jnp.concatenate([jnp.triu(jnp.ones((tm_mix, tm_mix), BF16)),
                           jnp.ones((tm_mix, LANES), BF16)], axis=1)
    eye = jnp.eye(n_blocks, dtype=F32)[:, None, :, None]

    def block_diag(w):
        return (w[:, :, None, :] * eye).reshape(d_rec, d_rec).astype(BF16)

    def mixer_weights(l):
        wl = w_in[l]
        w_rows = jnp.concatenate([wl[:, :o_k], wl[:, o_v:o_f], wl[:, o_xr:]], axis=-1).astype(BF16)
        w_f = jnp.pad(wl[:, o_f:o_xr], ((0, 0), (0, BF16_ROWS - n_heads)))
        w_cols = jnp.concatenate([wl[:, o_k:o_v], w_f], axis=-1).T.astype(BF16)
        bf_rep = jnp.broadcast_to(jnp.pad(b_f[l], (0, SUBLANES - n_heads))[:, None],
                                  (SUBLANES, LANES))
        return (w_rows, w_cols, bf_rep, tri, conv_w[l], conv_b[l][None, :],
                block_diag(w_rg_a[l]), b_rg_a[l][None, :], block_diag(w_rg_x[l]),
                b_rg_x[l][None, :], rg_lambda[l][None, :])

    w_out_b = w_out.astype(BF16)
    w_ffn_in_b = w_ffn_in.astype(BF16)
    w_ffn_out_b = w_ffn_out.astype(BF16)

    xf = x.reshape(b * s, d)
    for l in range(depth):
        g = norm_g[l][:, None, :]
        xf = _ffn(xf, g[0], w_ffn_in_b, w_ffn_out_b, (l, 0), tm=tm_ffn, chunks=chunks)
        q, kt, v, ct, y_rec = _mixer_in(
            xf.reshape(b, s, d), g[1], *mixer_weights(l),
            d_attn=d_attn, d_rec=d_rec, scale=LOG2E / math.sqrt(hd), tm=tm_mix)
        c_pairs = ct[:, :n_heads, :].reshape(b, n_heads // 2, 2, s)
        y_attn = _attention(q, kt, v, c_pairs, hd=hd, tq=tq)
        last = l == depth - 1
        xf = _ffn(xf, g[2], w_ffn_in_b, w_ffn_out_b, (l, 1),
                  proj=(y_attn.reshape(b * s, d_attn), y_rec.reshape(b * s, d_rec), w_out_b),
                  final_g=final_g[None, :] if last else None, tm=tm_ffn, chunks=chunks)
    return xf.reshape(b, s, d)
```
